```python
import jax, jax.numpy as jnp
from jax import lax
import numpy as np

D_MODEL = 1024
BATCH = 4
SEQ = 4096
DEPTH = 1

N_META = 16
N_Q_HEADS = 16
N_KV_HEADS = 2
HEAD_DIM = 64
GROUP = N_Q_HEADS // N_KV_HEADS
ROT_DIM = HEAD_DIM // 4
ROPE_THETA = 500000.0
WINDOW = 128
BLOCK = 128
ATTN_WIDTH = N_Q_HEADS * HEAD_DIM
KV_WIDTH = N_KV_HEADS * HEAD_DIM
CONV_CH = D_MODEL
CONV_K = 31
FFN_DIM = 2816
FFN_CONV_K = 3
IN_WIDTH = ATTN_WIDTH + 2 * KV_WIDTH + 2 * CONV_CH + 2 * D_MODEL
RMS_EPS = 1e-6
LN_EPS = 1e-5
NEG_INF = -1e30

kernel_name = "hybrid_swa_sink_conformer_convffn_block"


def rms_norm(x, g):
    xf = x.astype(jnp.float32)
    y = xf * lax.rsqrt(jnp.mean(xf * xf, axis=-1, keepdims=True) + RMS_EPS)
    return (y * g.astype(jnp.float32)).astype(x.dtype)


def layer_norm(x, g, b):
    xf = x.astype(jnp.float32)
    mu = jnp.mean(xf, axis=-1, keepdims=True)
    var = jnp.mean(jnp.square(xf - mu), axis=-1, keepdims=True)
    y = (xf - mu) * lax.rsqrt(var + LN_EPS)
    return (y * g.astype(jnp.float32) + b.astype(jnp.float32)).astype(x.dtype)


def causal_dwconv(x, w, b):
    k = w.shape[0]
    y = lax.conv_general_dilated(
        x, w[:, None, :].astype(x.dtype), window_strides=(1,), padding=[(k - 1, 0)],
        dimension_numbers=("NWC", "WIO", "NWC"), feature_group_count=x.shape[-1])
    return y + b.astype(x.dtype)


def partial_rope(x, pos):
    half = ROT_DIM // 2
    inv_freq = ROPE_THETA ** (-jnp.arange(half, dtype=jnp.float32) * 2.0 / ROT_DIM)
    ang = pos.astype(jnp.float32)[:, None] * inv_freq[None, :]
    cos = jnp.cos(ang)[None, :, None, :]
    sin = jnp.sin(ang)[None, :, None, :]
    xr = x[..., :ROT_DIM].astype(jnp.float32)
    x1, x2 = xr[..., :half], xr[..., half:]
    rot = jnp.concatenate([x1 * cos - x2 * sin, x2 * cos + x1 * sin], axis=-1).astype(x.dtype)
    return jnp.concatenate([rot, x[..., ROT_DIM:]], axis=-1)


def sliding_window_sink_attention(q, k, v, sinks):
    bsz, seq_len = q.shape[0], q.shape[1]
    pad = BLOCK - N_META
    padded = seq_len + pad
    nb = padded // BLOCK
    scale = HEAD_DIM ** -0.5

    def pad_front(a):
        return jnp.pad(a, ((0, 0), (pad, 0), (0, 0), (0, 0)))

    def shift_block(a):
        return jnp.concatenate([jnp.zeros_like(a[:, :1]), a[:, :-1]], axis=1)

    qb = (pad_front(q) * scale).reshape(bsz, nb, BLOCK, N_KV_HEADS, GROUP, HEAD_DIM)
    kb = pad_front(k).reshape(bsz, nb, BLOCK, N_KV_HEADS, HEAD_DIM)
    vb = pad_front(v).reshape(bsz, nb, BLOCK, N_KV_HEADS, HEAD_DIM)
    k_meta = jnp.broadcast_to(k[:, None, :N_META], (bsz, nb, N_META, N_KV_HEADS, HEAD_DIM))
    v_meta = jnp.broadcast_to(v[:, None, :N_META], (bsz, nb, N_META, N_KV_HEADS, HEAD_DIM))
    keys = jnp.concatenate([k_meta, shift_block(kb), kb], axis=2)
    vals = jnp.concatenate([v_meta, shift_block(vb), vb], axis=2)

    tpos = (jnp.arange(padded) - pad).reshape(nb, BLOCK)
    tq = tpos[:, :, None]
    t_meta = jnp.arange(N_META)[None, None, :]
    t_loc = jnp.concatenate([tpos - BLOCK, tpos], axis=1)[:, None, :]
    meta_ok = jnp.broadcast_to(t_meta <= tq, (nb, BLOCK, N_META))
    loc_ok = (t_loc >= N_META) & (t_loc <= tq) & (tq - t_loc < WINDOW)
    mask = jnp.concatenate([meta_ok, loc_ok], axis=-1)

    s = jnp.einsum("bnqhgd,bnkhd->bnhgqk", qb, keys).astype(jnp.float32)
    s = jnp.where(mask[None, :, None, None], s, NEG_INF)
    sink = sinks.astype(jnp.float32).reshape(N_KV_HEADS, GROUP)[None, None, :, :, None, None]
    sink = jnp.broadcast_to(sink, s.shape[:-1] + (1,))
    probs = jax.nn.softmax(jnp.concatenate([s, sink], axis=-1), axis=-1)[..., :-1]
    o = jnp.einsum("bnhgqk,bnkhd->bnqhgd", probs.astype(v.dtype), vals)
    return o.reshape(bsz, padded, ATTN_WIDTH)[:, pad:]


def hybrid_mixer(h, pos, w_in, b_in, attn_sinks, w_attn_proj, conv_dw_w, conv_dw_b,
                 conv_ln_g, conv_ln_b, w_conv_proj, b_conv_proj, w_out):
    bsz, seq_len = h.shape[0], h.shape[1]
    proj = h @ w_in + b_in
    cuts = np.cumsum([ATTN_WIDTH, KV_WIDTH, KV_WIDTH, 2 * CONV_CH, D_MODEL]).tolist()
    q, k, v, glu_in, gate_a, gate_c = jnp.split(proj, cuts, axis=-1)

    q = partial_rope(q.reshape(bsz, seq_len, N_Q_HEADS, HEAD_DIM), pos)
    k = partial_rope(k.reshape(bsz, seq_len, N_KV_HEADS, HEAD_DIM), pos)
    v = v.reshape(bsz, seq_len, N_KV_HEADS, HEAD_DIM)
    attn = sliding_window_sink_attention(q, k, v, attn_sinks) @ w_attn_proj

    a, g = jnp.split(glu_in, 2, axis=-1)
    c = causal_dwconv(a * jax.nn.sigmoid(g), conv_dw_w, conv_dw_b)
    c = jax.nn.silu(layer_norm(c, conv_ln_g, conv_ln_b))
    conv = c @ w_conv_proj + b_conv_proj

    merged = jax.nn.sigmoid(gate_a) * attn + jax.nn.sigmoid(gate_c) * conv
    return merged @ w_out


def conv_ffn(h, w_up, ffn_dw_w, ffn_dw_b, w_down):
    u = causal_dwconv(h @ w_up, ffn_dw_w, ffn_dw_b)
    gate, val = jnp.split(u, 2, axis=-1)
    return (jax.nn.silu(gate) * val) @ w_down


def setup_inputs(seed: int = 0) -> dict:
    key = jax.random.key(seed)
    ks = jax.random.split(key, 24)
    f32 = jnp.float32

    def nrm(k, shape, scale):
        return jax.random.normal(k, shape, f32) * scale

    def gain(k, shape):
        return 1.0 + 0.1 * jax.random.normal(k, shape, f32)

    L = DEPTH
    return {
        "x": nrm(ks[0], (BATCH, SEQ, D_MODEL), 1.0),
        "meta_tokens": nrm(ks[1], (N_META, D_MODEL), 1.0),
        "norm_pre_mix": gain(ks[2], (L, D_MODEL)),
        "norm_post_mix": gain(ks[3], (L, D_MODEL)),
        "w_in": nrm(ks[4], (L, D_MODEL, IN_WIDTH), D_MODEL ** -0.5),
        "b_in": nrm(ks[5], (L, IN_WIDTH), 0.02),
        "attn_sinks": nrm(ks[6], (L, N_Q_HEADS), 0.5),
        "w_attn_proj": nrm(ks[7], (L, ATTN_WIDTH, D_MODEL), ATTN_WIDTH ** -0.5),
        "conv_dw_w": nrm(ks[8], (L, CONV_K, CONV_CH), CONV_K ** -0.5),
        "conv_dw_b": nrm(ks[9], (L, CONV_CH), 0.02),
        "conv_ln_g": gain(ks[10], (L, CONV_CH)),
        "conv_ln_b": nrm(ks[11], (L, CONV_CH), 0.02),
        "w_conv_proj": nrm(ks[12], (L, CONV_CH, D_MODEL), CONV_CH ** -0.5),
        "b_conv_proj": nrm(ks[13], (L, D_MODEL), 0.02),
        "w_out": nrm(ks[14], (L, D_MODEL, D_MODEL), D_MODEL ** -0.5),
        "norm_pre_ffn": gain(ks[15], (L, D_MODEL)),
        "norm_post_ffn": gain(ks[16], (L, D_MODEL)),
        "w_up": nrm(ks[17], (L, D_MODEL, 2 * FFN_DIM), D_MODEL ** -0.5),
        "ffn_dw_w": nrm(ks[18], (L, FFN_CONV_K, 2 * FFN_DIM), FFN_CONV_K ** -0.5),
        "ffn_dw_b": nrm(ks[19], (L, 2 * FFN_DIM), 0.02),
        "w_down": nrm(ks[20], (L, FFN_DIM, D_MODEL), FFN_DIM ** -0.5),
    }


def reference(x, meta_tokens, norm_pre_mix, norm_post_mix, w_in, b_in, attn_sinks, w_attn_proj,
              conv_dw_w, conv_dw_b, conv_ln_g, conv_ln_b, w_conv_proj, b_conv_proj, w_out,
              norm_pre_ffn, norm_post_ffn, w_up, ffn_dw_w, ffn_dw_b, w_down):
    bsz = x.shape[0]
    meta = jnp.broadcast_to(meta_tokens.astype(x.dtype)[None], (bsz, N_META, D_MODEL))
    h = jnp.concatenate([meta, x], axis=1)
    pos = jnp.arange(h.shape[1])
    for l in range(DEPTH):
        mix = hybrid_mixer(rms_norm(h, norm_pre_mix[l]), pos, w_in[l], b_in[l], attn_sinks[l],
                           w_attn_proj[l], conv_dw_w[l], conv_dw_b[l], conv_ln_g[l], conv_ln_b[l],
                           w_conv_proj[l], b_conv_proj[l], w_out[l])
        h = h + rms_norm(mix, norm_post_mix[l])
        ffn = conv_ffn(rms_norm(h, norm_pre_ffn[l]), w_up[l], ffn_dw_w[l], ffn_dw_b[l], w_down[l])
        h = h + rms_norm(ffn, norm_post_ffn[l])
    return h[:, N_META:]
```

```python
import functools

import numpy as np
import jax
import jax.numpy as jnp
from jax import lax
from jax.experimental import pallas as pl
from jax.experimental.pallas import tpu as pltpu

D_MODEL = 1024
N_META = 16
N_Q_HEADS = 16
N_KV_HEADS = 2
HEAD_DIM = 64
GROUP = N_Q_HEADS // N_KV_HEADS
ROT_DIM = HEAD_DIM // 4
ROT_HALF = ROT_DIM // 2
ROPE_THETA = 500000.0
WINDOW = 128
BLOCK = 128
ATTN_WIDTH = N_Q_HEADS * HEAD_DIM
KV_WIDTH = N_KV_HEADS * HEAD_DIM
CONV_CH = D_MODEL
CONV_K = 31
FFN_DIM = 2816
FFN_CONV_K = 3
RMS_EPS = 1e-6
LN_EPS = 1e-5
NEG_INF = -1e30

LANES = 128
SUBLANES = 8
MIX_TILE = 256
FFN_TILE = 512
FFN_CHUNK = 256
N_FFN_CHUNKS = FFN_DIM // FFN_CHUNK
CONV_HALO = 32
CONV_SHIFTS = SUBLANES
CONV_PAD = CONV_HALO - SUBLANES
VMEM_LIMIT = 56 * 1024 * 1024

F32 = jnp.float32
BF16 = jnp.bfloat16

_NT = (((1,), (1,)), ((), ()))


def _dot(a, b):
    return jnp.dot(a, b, preferred_element_type=F32)


def _dot_nt(a, b):
    return lax.dot_general(a, b, _NT, preferred_element_type=F32)


def _rms(x, g):
    ms = jnp.mean(x * x, axis=-1, keepdims=True)
    return x * lax.rsqrt(ms + RMS_EPS) * g


def _layer_norm(x, g, b):
    mu = jnp.mean(x, axis=-1, keepdims=True)
    xc = x - mu
    var = jnp.mean(xc * xc, axis=-1, keepdims=True)
    return xc * lax.rsqrt(var + LN_EPS) * g + b


def _silu(x):
    return x * jax.nn.sigmoid(x)


def _rope_rows(z, c, s1, s2):
    parts = []
    for g in range(z.shape[1] // LANES):
        zg = z[:, g * LANES:(g + 1) * LANES]
        parts.append(zg * c + pltpu.roll(zg, LANES - ROT_HALF, 1) * s1
                     + pltpu.roll(zg, ROT_HALF, 1) * s2)
    return parts[0] if len(parts) == 1 else jnp.concatenate(parts, axis=1)


def _key_variants(k):
    lane = lax.broadcasted_iota(jnp.int32, k.shape, 1)
    lo = lane < HEAD_DIM
    kr = pltpu.roll(k, HEAD_DIM, 1)
    zero = jnp.zeros_like(k)
    return [jnp.where(lo, k, zero).astype(BF16), jnp.where(lo, zero, kr).astype(BF16),
            jnp.where(lo, kr, zero).astype(BF16), jnp.where(lo, zero, k).astype(BF16)]


def _meta_kernel(meta_ref, gpre_ref, wqT_ref, bq_ref, wk_ref, bk_ref, wvT_ref, bv_ref,
                 wglu_ref, bglu_ref, wgate_ref, bgate_ref, sink_ref, wap_ref, dww_ref,
                 dwb_ref, lng_ref, lnb_ref, wcp_ref, bcp_ref, wout_ref, gpost_ref,
                 gffn_ref, wup_ref, rc_ref, rs1_ref, rs2_ref,
                 k_out, v_out, glu_out, u_out, gscr):
    m = meta_ref[...]
    hn = _rms(m, gpre_ref[...]).astype(BF16)
    q = _dot_nt(hn, wqT_ref[...]) + bq_ref[...]
    k = _dot(hn, wk_ref[...]) + bk_ref[...]
    v = _dot_nt(hn, wvT_ref[...]) + bv_ref[...]
    rc, rs1, rs2 = rc_ref[...], rs1_ref[...], rs2_ref[...]
    q = _rope_rows(q, rc, rs1, rs2)
    k = _rope_rows(k, rc, rs1, rs2)
    k_out[...] = k
    v_out[...] = v

    row = lax.broadcasted_iota(jnp.int32, (N_META, N_META), 0)
    col = lax.broadcasted_iota(jnp.int32, (N_META, N_META), 1)
    causal = col <= row
    qb, kb, vb = q.astype(BF16), k.astype(BF16), v.astype(BF16)
    attn = jnp.zeros((N_META, D_MODEL), F32)
    for hq in range(N_Q_HEADS):
        h = hq // GROUP
        qh = qb[:, hq * HEAD_DIM:(hq + 1) * HEAD_DIM]
        kh = kb[:, h * HEAD_DIM:(h + 1) * HEAD_DIM]
        vh = vb[:, h * HEAD_DIM:(h + 1) * HEAD_DIM]
        s = jnp.where(causal, _dot_nt(qh, kh), NEG_INF)
        sink = sink_ref[hq]
        mx = jnp.maximum(jnp.max(s, axis=-1, keepdims=True), sink)
        p = jnp.exp(s - mx)
        den = jnp.sum(p, axis=-1, keepdims=True) + jnp.exp(sink - mx)
        o = _dot(p.astype(BF16), vh) / den
        attn = attn + _dot(o.astype(BF16), wap_ref[hq * HEAD_DIM:(hq + 1) * HEAD_DIM, :])

    glu_in = _dot(hn, wglu_ref[...]) + bglu_ref[...]
    glu = glu_in[:, :CONV_CH] * jax.nn.sigmoid(glu_in[:, CONV_CH:])
    glu_out[...] = glu
    gscr[0:CONV_HALO, :] = jnp.zeros((CONV_HALO, CONV_CH), F32)
    gscr[CONV_HALO:CONV_HALO + N_META, :] = glu
    c = jnp.broadcast_to(dwb_ref[...], (N_META, CONV_CH))
    off = CONV_HALO - (CONV_K - 1)
    for kk in range(CONV_K):
        c = c + gscr[off + kk:off + kk + N_META, :] * dww_ref[kk:kk + 1, :]
    c = _silu(_layer_norm(c, lng_ref[...], lnb_ref[...]))
    conv = _dot(c.astype(BF16), wcp_ref[...]) + bcp_ref[...]

    gates = jax.nn.sigmoid(_dot(hn, wgate_ref[...]) + bgate_ref[...])
    merged = gates[:, :D_MODEL] * attn + gates[:, D_MODEL:] * conv
    mix = _dot(merged.astype(BF16), wout_ref[...])
    h2 = m + _rms(mix, gpost_ref[...])
    hn2 = _rms(h2, gffn_ref[...]).astype(BF16)
    for cidx in range(2 * N_FFN_CHUNKS):
        u_out[cidx] = _dot(hn2, wup_ref[cidx])


def _mixer_kernel(x_ref, gpre_ref, wqT_ref, bqT_ref, wk_ref, bk_ref, wvT_ref, bvT_ref,
                  wglu_ref, bglu_ref, wgate_ref, bgate_ref, sink_ref, wap_ref, dww_ref,
                  dwb_ref, lng_ref, lnb_ref, wcp_ref, bcp_ref, wout_ref, gpost_ref,
                  rc_ref, rs1_ref, rs2_ref, cosT_ref, sinT_ref,
                  kmeta_ref, vmetaT_ref, glumeta_ref,
                  out_ref,
                  kbuf, vTbuf, gbuf, shbuf, qT_s, oT_s, attn_s, cbuf, bias_s):
    T = MIX_TILE
    nblk = T // BLOCK
    t = pl.program_id(1)

    @pl.when(t == 0)
    def _init():
        kbuf[:, 0:BLOCK, :] = jnp.zeros((4, BLOCK, LANES), BF16)
        vTbuf[:, 0:BLOCK] = jnp.zeros((KV_WIDTH, BLOCK), BF16)
        gbuf[0:CONV_HALO - N_META, :] = jnp.zeros((CONV_HALO - N_META, CONV_CH), F32)
        gbuf[CONV_HALO - N_META:CONV_HALO, :] = glumeta_ref[...]
        key = lax.broadcasted_iota(jnp.int32, (2 * BLOCK, 2 * BLOCK), 0)
        qry = lax.broadcasted_iota(jnp.int32, (2 * BLOCK, 2 * BLOCK), 1) % BLOCK
        vis = (key > qry) & (key <= qry + WINDOW)
        bias_s[0] = jnp.where(vis, 0.0, NEG_INF).astype(F32)
        bias_s[1] = jnp.where(vis & (key >= BLOCK), 0.0, NEG_INF).astype(F32)

    x = x_ref[0]
    hn = _rms(x, gpre_ref[...]).astype(BF16)

    qT = _dot_nt(wqT_ref[...], hn) + bqT_ref[...]
    cosT, sinT = cosT_ref[...], sinT_ref[...]
    for hq in range(N_Q_HEADS):
        base = hq * HEAD_DIM
        r0 = qT[base:base + ROT_HALF]
        r1 = qT[base + ROT_HALF:base + ROT_DIM]
        rot = jnp.concatenate([r0 * cosT - r1 * sinT, r1 * cosT + r0 * sinT], axis=0)
        qT_s[base:base + ROT_DIM, :] = rot.astype(BF16)
        qT_s[base + ROT_DIM:base + HEAD_DIM, :] = qT[base + ROT_DIM:base + HEAD_DIM].astype(BF16)

    k = _dot(hn, wk_ref[...]) + bk_ref[...]
    k = _rope_rows(k, rc_ref[...], rs1_ref[...], rs2_ref[...])
    for i, kv in enumerate(_key_variants(k)):
        kbuf[i, BLOCK:BLOCK + T, :] = kv
    kmeta_var = _key_variants(kmeta_ref[...])
    vT = _dot_nt(wvT_ref[...], hn) + bvT_ref[...]
    vTbuf[:, BLOCK:BLOCK + T] = vT.astype(BF16)
    vmetaT = vmetaT_ref[...]

    lane2 = lax.broadcasted_iota(jnp.int32, (1, 2 * BLOCK), 1)
    for j in range(nblk):
        if j == 0:
            bias = bias_s[jnp.where(t == 0, 1, 0)]
        else:
            bias = bias_s[0]
        c0 = j * BLOCK
        for gp in range(4):
            h = gp // 2
            rhs = jnp.concatenate(
                [qT_s[256 * gp:256 * gp + 128, c0:c0 + BLOCK],
                 qT_s[256 * gp + 128:256 * gp + 256, c0:c0 + BLOCK]], axis=1)
            vT_h = vTbuf[h * HEAD_DIM:(h + 1) * HEAD_DIM, c0:c0 + 2 * BLOCK]
            vmT_h = vmetaT[h * HEAD_DIM:(h + 1) * HEAD_DIM, :]
            for hl in range(2):
                ha = 4 * gp + hl
                hb = ha + 2
                s_loc = _dot(kbuf[2 * h + hl, c0:c0 + 2 * BLOCK, :], rhs) + bias
                s_met = _dot(kmeta_var[2 * h + hl], rhs)
                sink = jnp.where(lane2 < BLOCK, sink_ref[ha], sink_ref[hb])
                mx = jnp.maximum(jnp.maximum(jnp.max(s_loc, axis=0, keepdims=True),
                                             jnp.max(s_met, axis=0, keepdims=True)), sink)
                p_loc = jnp.exp(s_loc - mx)
                p_met = jnp.exp(s_met - mx)
                den = (jnp.sum(p_loc, axis=0, keepdims=True)
                       + jnp.sum(p_met, axis=0, keepdims=True) + jnp.exp(sink - mx))
                oT = _dot(vT_h, p_loc.astype(BF16)) + _dot(vmT_h, p_met.astype(BF16))
                oT = oT * (1.0 / den)
                oT_s[ha * HEAD_DIM:(ha + 1) * HEAD_DIM, :] = oT[:, :BLOCK]
                oT_s[hb * HEAD_DIM:(hb + 1) * HEAD_DIM, :] = oT[:, BLOCK:]
        o_blk = oT_s[...].T.astype(BF16)
        attn_s[c0:c0 + BLOCK, :] = _dot(o_blk, wap_ref[...])

    kbuf[:, 0:BLOCK, :] = kbuf[:, T:T + BLOCK, :]
    vTbuf[:, 0:BLOCK] = vTbuf[:, T:T + BLOCK]

    glu_in = _dot(hn, wglu_ref[...]) + bglu_ref[...]
    gbuf[CONV_HALO:CONV_HALO + T, :] = glu_in[:, :CONV_CH] * jax.nn.sigmoid(glu_in[:, CONV_CH:])
    for s in range(1, CONV_SHIFTS):
        shbuf[s - 1] = gbuf[SUBLANES - s:SUBLANES - s + T + CONV_PAD, :]

    def conv_rows(rc, carry):
        r0 = pl.multiple_of(rc * SUBLANES, SUBLANES)
        acc = jnp.broadcast_to(dwb_ref[...], (SUBLANES, CONV_CH))
        for d in range(CONV_K):
            a, s = divmod(d, SUBLANES)
            w = dww_ref[CONV_K - 1 - d:CONV_K - d, :]
            if s == 0:
                g = gbuf[pl.ds(r0 + CONV_HALO - SUBLANES * a, SUBLANES), :]
            else:
                g = shbuf[s - 1, pl.ds(r0 + CONV_PAD - SUBLANES * a, SUBLANES), :]
            acc = acc + g * w
        cbuf[pl.ds(r0, SUBLANES), :] = acc
        return carry

    lax.fori_loop(0, T // SUBLANES, conv_rows, 0)
    gbuf[0:CONV_HALO, :] = gbuf[T:T + CONV_HALO, :]

    c = _silu(_layer_norm(cbuf[...], lng_ref[...], lnb_ref[...]))
    conv = _dot(c.astype(BF16), wcp_ref[...]) + bcp_ref[...]

    gates = jax.nn.sigmoid(_dot(hn, wgate_ref[...]) + bgate_ref[...])
    merged = gates[:, :D_MODEL] * attn_s[...] + gates[:, D_MODEL:] * conv
    mix = _dot(merged.astype(BF16), wout_ref[...])
    out_ref[0] = x + _rms(mix, gpost_ref[...])


def _ffn_kernel(h_ref, gpre_ref, gpost_ref, wup_ref, dwp_ref, wdown_ref, umeta_ref,
                out_ref, hn_s, ubuf, carry, acc_s):
    T = FFN_TILE
    t = pl.program_id(1)

    @pl.when(t == 0)
    def _init():
        carry[...] = umeta_ref[...]

    h = h_ref[0]
    hn_s[...] = _rms(h, gpre_ref[...]).astype(BF16)
    acc_s[...] = jnp.zeros((T, D_MODEL), F32)

    def chunk(cidx, carry_unused):
        ys = []
        for half in range(2):
            idx = cidx + N_FFN_CHUNKS * half
            u = _dot(hn_s[...], wup_ref[idx])
            ubuf[half, 0:SUBLANES, :] = carry[idx]
            ubuf[half, SUBLANES:SUBLANES + T, :] = u
            carry[idx] = u[T - SUBLANES:T, :]
            w = dwp_ref[idx]
            y = (w[3:4, :] + w[0:1, :] * ubuf[half, SUBLANES - 2:SUBLANES - 2 + T, :]
                 + w[1:2, :] * ubuf[half, SUBLANES - 1:SUBLANES - 1 + T, :]
                 + w[2:3, :] * u)
            ys.append(y)
        act = (_silu(ys[0]) * ys[1]).astype(BF16)
        acc_s[...] += _dot(act, wdown_ref[cidx])
        return carry_unused

    lax.fori_loop(0, N_FFN_CHUNKS, chunk, 0)
    out_ref[0] = h + _rms(acc_s[...], gpost_ref[...])


def _const_spec(shape):
    nd = len(shape)
    return pl.BlockSpec(shape, lambda *_: (0,) * nd, pipeline_mode=pl.Buffered(1))


def _rope_tables(n_pos):
    inv_freq = ROPE_THETA ** (-jnp.arange(ROT_HALF, dtype=F32) * 2.0 / ROT_DIM)
    ang = jnp.arange(n_pos).astype(F32)[:, None] * inv_freq[None, :]
    cos, sin = jnp.cos(ang), jnp.sin(ang)
    ones = jnp.ones((n_pos, HEAD_DIM - ROT_DIM), F32)
    zeros = jnp.zeros((n_pos, HEAD_DIM - ROT_HALF), F32)
    c = jnp.concatenate([cos, cos, ones], axis=1)
    s1 = jnp.concatenate([-sin, zeros], axis=1)
    s2 = jnp.concatenate([zeros[:, :ROT_HALF], sin, zeros[:, :HEAD_DIM - ROT_DIM]], axis=1)
    rep = LANES // HEAD_DIM
    return (jnp.tile(c, (1, rep)), jnp.tile(s1, (1, rep)), jnp.tile(s2, (1, rep)),
            cos.T, sin.T)


def kernel(x, meta_tokens, norm_pre_mix, norm_post_mix, w_in, b_in, attn_sinks, w_attn_proj,
           conv_dw_w, conv_dw_b, conv_ln_g, conv_ln_b, w_conv_proj, b_conv_proj, w_out,
           norm_pre_ffn, norm_post_ffn, w_up, ffn_dw_w, ffn_dw_b, w_down):
    bsz, seq, _ = x.shape
    assert seq % MIX_TILE == 0 and seq % FFN_TILE == 0
    scale = HEAD_DIM ** -0.5
    row = lambda v: v.reshape(1, -1).astype(F32)

    wi, bi = w_in[0], b_in[0]
    c_q, c_k, c_v, c_glu = ATTN_WIDTH, ATTN_WIDTH + KV_WIDTH, ATTN_WIDTH + 2 * KV_WIDTH, \
        ATTN_WIDTH + 2 * KV_WIDTH + 2 * CONV_CH
    wqT = (wi[:, :c_q] * scale).T.astype(BF16)
    bq = bi[:c_q] * scale
    wk = wi[:, c_q:c_k].astype(BF16)
    bk = row(bi[c_q:c_k])
    wvT = wi[:, c_k:c_v].T.astype(BF16)
    bv = bi[c_k:c_v]
    wglu = wi[:, c_v:c_glu].astype(BF16)
    bglu = row(bi[c_v:c_glu])
    wgate = wi[:, c_glu:].astype(BF16)
    bgate = row(bi[c_glu:])
    sinks = attn_sinks[0].astype(F32)
    wap = w_attn_proj[0].astype(BF16)
    dww = conv_dw_w[0].astype(F32)
    dwb = row(conv_dw_b[0])
    lng, lnb = row(conv_ln_g[0]), row(conv_ln_b[0])
    wcp = w_conv_proj[0].astype(BF16)
    bcp = row(b_conv_proj[0])
    wout = w_out[0].astype(BF16)
    gpre, gpost = row(norm_pre_mix[0]), row(norm_post_mix[0])
    gffn, gffn_post = row(norm_pre_ffn[0]), row(norm_post_ffn[0])

    nch = 2 * N_FFN_CHUNKS
    wup = w_up[0].reshape(D_MODEL, nch, FFN_CHUNK).transpose(1, 0, 2).astype(BF16)
    dwp = jnp.concatenate(
        [ffn_dw_w[0], ffn_dw_b[0][None, :], jnp.zeros((SUBLANES - FFN_CONV_K - 1, 2 * FFN_DIM), F32)],
        axis=0).reshape(SUBLANES, nch, FFN_CHUNK).transpose(1, 0, 2)
    wdown = w_down[0].reshape(N_FFN_CHUNKS, FFN_CHUNK, D_MODEL).astype(BF16)

    rc, rs1, rs2, cosT, sinT = _rope_tables(N_META + seq)

    smem = pl.BlockSpec(memory_space=pltpu.SMEM)
    vmem = pl.BlockSpec(memory_space=pltpu.VMEM)

    k_meta, v_meta, glu_meta, u_meta = pl.pallas_call(
        _meta_kernel,
        out_shape=(jax.ShapeDtypeStruct((N_META, KV_WIDTH), F32),
                   jax.ShapeDtypeStruct((N_META, KV_WIDTH), F32),
                   jax.ShapeDtypeStruct((N_META, CONV_CH), F32),
                   jax.ShapeDtypeStruct((nch, N_META, FFN_CHUNK), F32)),
        in_specs=[vmem] * 12 + [smem] + [vmem] * 14,
        out_specs=(vmem, vmem, vmem, vmem),
        scratch_shapes=[pltpu.VMEM((CONV_HALO + N_META, CONV_CH), F32)],
        compiler_params=pltpu.CompilerParams(vmem_limit_bytes=VMEM_LIMIT),
        name="meta_prologue",
    )(meta_tokens.astype(F32), gpre, wqT, row(bq), wk, bk, wvT, row(bv), wglu, bglu, wgate,
      bgate, sinks, wap, dww, dwb, lng, lnb, wcp, bcp, wout, gpost, gffn, wup,
      rc[:N_META], rs1[:N_META], rs2[:N_META])

    vmetaT = v_meta.T.astype(BF16)
    umeta_tail = u_meta[:, N_META - SUBLANES:, :]

    T = MIX_TILE
    bqT = jnp.broadcast_to(bq[:, None], (ATTN_WIDTH, T)).astype(F32)
    bvT = jnp.broadcast_to(bv[:, None], (KV_WIDTH, T)).astype(F32)
    tile_spec = pl.BlockSpec((1, T, D_MODEL), lambda b, t: (b, t, 0))
    rope_spec = pl.BlockSpec((T, LANES), lambda b, t: (t, 0))
    ropeT_spec = pl.BlockSpec((SUBLANES, T), lambda b, t: (0, t))
    mixer_in = [
        (x, tile_spec), (gpre, None), (wqT, None), (bqT, None), (wk, None), (bk, None),
        (wvT, None), (bvT, None), (wglu, None), (bglu, None), (wgate, None), (bgate, None),
        (sinks, smem), (wap, None), (dww, None), (dwb, None), (lng, None), (lnb, None),
        (wcp, None), (bcp, None), (wout, None), (gpost, None),
        (rc[N_META:], rope_spec), (rs1[N_META:], rope_spec), (rs2[N_META:], rope_spec),
        (cosT[:, N_META:], ropeT_spec), (sinT[:, N_META:], ropeT_spec),
        (k_meta, None), (vmetaT, None), (glu_meta, None),
    ]
    h2 = pl.pallas_call(
        _mixer_kernel,
        out_shape=jax.ShapeDtypeStruct((bsz, seq, D_MODEL), F32),
        grid=(bsz, seq // T),
        in_specs=[spec if spec is not None else _const_spec(a.shape) for a, spec in mixer_in],
        out_specs=tile_spec,
        scratch_shapes=[
            pltpu.VMEM((4, BLOCK + T, LANES), BF16),
            pltpu.VMEM((KV_WIDTH, BLOCK + T), BF16),
            pltpu.VMEM((CONV_HALO + T, CONV_CH), F32),
            pltpu.VMEM((CONV_SHIFTS - 1, T + CONV_PAD, CONV_CH), F32),
            pltpu.VMEM((ATTN_WIDTH, T), BF16),
            pltpu.VMEM((ATTN_WIDTH, BLOCK), F32),
            pltpu.VMEM((T, D_MODEL), F32),
            pltpu.VMEM((T, CONV_CH), F32),
            pltpu.VMEM((2, 2 * BLOCK, 2 * BLOCK), F32),
        ],
        compiler_params=pltpu.CompilerParams(
            dimension_semantics=("arbitrary", "arbitrary"), vmem_limit_bytes=VMEM_LIMIT),
        name="mixer",
    )(*[a for a, _ in mixer_in])

    T2 = FFN_TILE
    tile2 = pl.BlockSpec((1, T2, D_MODEL), lambda b, t: (b, t, 0))
    ffn_in = [(h2, tile2), (gffn, None), (gffn_post, None), (wup, None), (dwp, None),
              (wdown, None), (umeta_tail, None)]
    out = pl.pallas_call(
        _ffn_kernel,
        out_shape=jax.ShapeDtypeStruct((bsz, seq, D_MODEL), F32),
        grid=(bsz, seq // T2),
        in_specs=[spec if spec is not None else _const_spec(a.shape) for a, spec in ffn_in],
        out_specs=tile2,
        scratch_shapes=[
            pltpu.VMEM((T2, D_MODEL), BF16),
            pltpu.VMEM((2, SUBLANES + T2, FFN_CHUNK), F32),
            pltpu.VMEM((nch, SUBLANES, FFN_CHUNK), F32),
            pltpu.VMEM((T2, D_MODEL), F32),
        ],
        compiler_params=pltpu.CompilerParams(
            dimension_semantics=("arbitrary", "arbitrary"), vmem_limit_bytes=VMEM_LIMIT),
        name="conv_ffn",
    )(*[a for a, _ in ffn_in])
    return out.astype(x.dtype)
```

```python
import functools

import numpy as np
import jax
import jax.numpy as jnp
from jax import lax
from jax.experimental import pallas as pl
from jax.experimental.pallas import tpu as pltpu

D_MODEL = 1024
N_META = 16
N_Q_HEADS = 16
N_KV_HEADS = 2
HEAD_DIM = 64
GROUP = N_Q_HEADS // N_KV_HEADS
ROT_DIM = HEAD_DIM // 4
ROT_HALF = ROT_DIM // 2
ROPE_THETA = 500000.0
WINDOW = 128
BLOCK = 128
ATTN_WIDTH = N_Q_HEADS * HEAD_DIM
KV_WIDTH = N_KV_HEADS * HEAD_DIM
CONV_CH = D_MODEL
CONV_K = 31
FFN_DIM = 2816
FFN_CONV_K = 3
RMS_EPS = 1e-6
LN_EPS = 1e-5
NEG_INF = -1e30

LANES = 128
SUBLANES = 8
MIX_TILE = 256
FFN_TILE = 512
FFN_CHUNK = 256
N_FFN_CHUNKS = FFN_DIM // FFN_CHUNK
FFN_DOWN_GROUPS = ((0, 4), (4, 8), (8, 11))
CONV_HALO = 32
CONV_SHIFTS = SUBLANES
CONV_PAD = CONV_HALO - SUBLANES
CONV_ROWS = 32
CONV_COLS = 256
ATTN_LOOKAHEAD = 2
VMEM_LIMIT = 56 * 1024 * 1024

F32 = jnp.float32
BF16 = jnp.bfloat16

_NT = (((1,), (1,)), ((), ()))


def _dot(a, b):
    return jnp.dot(a, b, preferred_element_type=F32)


def _dot_nt(a, b):
    return lax.dot_general(a, b, _NT, preferred_element_type=F32)


def _rms(x, g):
    ms = jnp.mean(x * x, axis=-1, keepdims=True)
    return x * lax.rsqrt(ms + RMS_EPS) * g


def _layer_norm(x, g, b):
    mu = jnp.mean(x, axis=-1, keepdims=True)
    xc = x - mu
    var = jnp.mean(xc * xc, axis=-1, keepdims=True)
    return xc * lax.rsqrt(var + LN_EPS) * g + b


def _silu(x):
    return x * jax.nn.sigmoid(x)


def _rope_rows(z, c, s1, s2):
    parts = []
    for g in range(z.shape[1] // LANES):
        zg = z[:, g * LANES:(g + 1) * LANES]
        parts.append(zg * c + pltpu.roll(zg, LANES - ROT_HALF, 1) * s1
                     + pltpu.roll(zg, ROT_HALF, 1) * s2)
    return parts[0] if len(parts) == 1 else jnp.concatenate(parts, axis=1)


def _key_variants(k):
    lane = lax.broadcasted_iota(jnp.int32, k.shape, 1)
    lo = lane < HEAD_DIM
    kr = pltpu.roll(k, HEAD_DIM, 1)
    zero = jnp.zeros_like(k)
    return [jnp.where(lo, k, zero).astype(BF16), jnp.where(lo, zero, kr).astype(BF16),
            jnp.where(lo, kr, zero).astype(BF16), jnp.where(lo, zero, k).astype(BF16)]


def _meta_kernel(meta_ref, gpre_ref, wqT_ref, bq_ref, wk_ref, bk_ref, wvT_ref, bv_ref,
                 wglu_ref, bglu_ref, wgate_ref, bgate_ref, sink_ref, wap_ref, dww_ref,
                 dwb_ref, lng_ref, lnb_ref, wcp_ref, bcp_ref, wout_ref, gpost_ref,
                 gffn_ref, wup_ref, rc_ref, rs1_ref, rs2_ref,
                 k_out, v_out, glu_out, u_out, gscr):
    m = meta_ref[...]
    hn = _rms(m, gpre_ref[...]).astype(BF16)
    q = _dot_nt(hn, wqT_ref[...]) + bq_ref[...]
    k = _dot(hn, wk_ref[...]) + bk_ref[...]
    v = _dot_nt(hn, wvT_ref[...]) + bv_ref[...]
    rc, rs1, rs2 = rc_ref[...], rs1_ref[...], rs2_ref[...]
    q = _rope_rows(q, rc, rs1, rs2)
    k = _rope_rows(k, rc, rs1, rs2)
    k_out[...] = k
    v_out[...] = v

    row = lax.broadcasted_iota(jnp.int32, (N_META, N_META), 0)
    col = lax.broadcasted_iota(jnp.int32, (N_META, N_META), 1)
    causal = col <= row
    qb, kb, vb = q.astype(BF16), k.astype(BF16), v.astype(BF16)
    attn = jnp.zeros((N_META, D_MODEL), F32)
    for hq in range(N_Q_HEADS):
        h = hq // GROUP
        qh = qb[:, hq * HEAD_DIM:(hq + 1) * HEAD_DIM]
        kh = kb[:, h * HEAD_DIM:(h + 1) * HEAD_DIM]
        vh = vb[:, h * HEAD_DIM:(h + 1) * HEAD_DIM]
        s = jnp.where(causal, _dot_nt(qh, kh), NEG_INF)
        sink = sink_ref[hq]
        mx = jnp.maximum(jnp.max(s, axis=-1, keepdims=True), sink)
        p = jnp.exp(s - mx)
        den = jnp.sum(p, axis=-1, keepdims=True) + jnp.exp(sink - mx)
        o = _dot(p.astype(BF16), vh) / den
        attn = attn + _dot(o.astype(BF16), wap_ref[hq * HEAD_DIM:(hq + 1) * HEAD_DIM, :])

    glu_in = _dot(hn, wglu_ref[...]) + bglu_ref[...]
    glu = glu_in[:, :CONV_CH] * jax.nn.sigmoid(glu_in[:, CONV_CH:])
    glu_out[...] = glu
    gscr[0:CONV_HALO, :] = jnp.zeros((CONV_HALO, CONV_CH), F32)
    gscr[CONV_HALO:CONV_HALO + N_META, :] = glu
    c = jnp.broadcast_to(dwb_ref[...], (N_META, CONV_CH))
    off = CONV_HALO - (CONV_K - 1)
    for kk in range(CONV_K):
        c = c + gscr[off + kk:off + kk + N_META, :] * dww_ref[kk:kk + 1, :]
    c = _silu(_layer_norm(c, lng_ref[...], lnb_ref[...]))
    conv = _dot(c.astype(BF16), wcp_ref[...]) + bcp_ref[...]

    gates = jax.nn.sigmoid(_dot(hn, wgate_ref[...]) + bgate_ref[...])
    merged = gates[:, :D_MODEL] * attn + gates[:, D_MODEL:] * conv
    mix = _dot(merged.astype(BF16), wout_ref[...])
    h2 = m + _rms(mix, gpost_ref[...])
    hn2 = _rms(h2, gffn_ref[...]).astype(BF16)
    for cidx in range(2 * N_FFN_CHUNKS):
        u_out[cidx] = _dot(hn2, wup_ref[cidx])


def _mixer_kernel(x_ref, gpre_ref, wqT_ref, bqT_ref, wk_ref, bk_ref, wvT_ref, bvT_ref,
                  wglu_ref, bglu_ref, wgate_ref, bgate_ref, sink_ref, wap_ref, dww_ref,
                  dwb_ref, lng_ref, lnb_ref, wcp_ref, bcp_ref, wout_ref, gpost_ref,
                  rc_ref, rs1_ref, rs2_ref, cosT_ref, sinT_ref,
                  kmeta_ref, vmetaT_ref, glumeta_ref,
                  out_ref,
                  kbuf, vTbuf, gbuf, shbuf, qT_s, oT_s, attn_s, cbuf, bias_s):
    T = MIX_TILE
    nblk = T // BLOCK
    t = pl.program_id(1)

    @pl.when(t == 0)
    def _init():
        kbuf[:, 0:BLOCK, :] = jnp.zeros((4, BLOCK, LANES), BF16)
        vTbuf[:, 0:BLOCK] = jnp.zeros((KV_WIDTH, BLOCK), BF16)
        gbuf[0:CONV_HALO - N_META, :] = jnp.zeros((CONV_HALO - N_META, CONV_CH), F32)
        gbuf[CONV_HALO - N_META:CONV_HALO, :] = glumeta_ref[...]
        key = lax.broadcasted_iota(jnp.int32, (2 * BLOCK, 2 * BLOCK), 0)
        qry = lax.broadcasted_iota(jnp.int32, (2 * BLOCK, 2 * BLOCK), 1) % BLOCK
        vis = (key > qry) & (key <= qry + WINDOW)
        bias_s[0] = jnp.where(vis, 0.0, NEG_INF).astype(F32)
        bias_s[1] = jnp.where(vis & (key >= BLOCK), 0.0, NEG_INF).astype(F32)

    x = x_ref[0]
    hn = _rms(x, gpre_ref[...]).astype(BF16)

    qT = _dot_nt(wqT_ref[...], hn) + bqT_ref[...]
    cosT, sinT = cosT_ref[...], sinT_ref[...]
    for hq in range(N_Q_HEADS):
        base = hq * HEAD_DIM
        r0 = qT[base:base + ROT_HALF]
        r1 = qT[base + ROT_HALF:base + ROT_DIM]
        rot = jnp.concatenate([r0 * cosT - r1 * sinT, r1 * cosT + r0 * sinT], axis=0)
        qT_s[base:base + ROT_DIM, :] = rot.astype(BF16)
        qT_s[base + ROT_DIM:base + HEAD_DIM, :] = qT[base + ROT_DIM:base + HEAD_DIM].astype(BF16)

    k = _dot(hn, wk_ref[...]) + bk_ref[...]
    k = _rope_rows(k, rc_ref[...], rs1_ref[...], rs2_ref[...])
    for i, kv in enumerate(_key_variants(k)):
        kbuf[i, BLOCK:BLOCK + T, :] = kv
    kmeta_var = _key_variants(kmeta_ref[...])
    vT = _dot_nt(wvT_ref[...], hn) + bvT_ref[...]
    vTbuf[:, BLOCK:BLOCK + T] = vT.astype(BF16)
    vmetaT = vmetaT_ref[...]

    lane2 = lax.broadcasted_iota(jnp.int32, (1, 2 * BLOCK), 1)
    first_sel = jnp.where(t == 0, 1, 0)

    items = [(j, gp, hl) for j in range(nblk) for gp in range(4) for hl in range(2)]

    def scores(item):
        j, gp, hl = item
        h = gp // 2
        c0 = j * BLOCK
        rhs = jnp.concatenate(
            [qT_s[256 * gp:256 * gp + 128, c0:c0 + BLOCK],
             qT_s[256 * gp + 128:256 * gp + 256, c0:c0 + BLOCK]], axis=1)
        s_loc = _dot(kbuf[2 * h + hl, c0:c0 + 2 * BLOCK, :], rhs)
        s_met = _dot(kmeta_var[2 * h + hl], rhs)
        return s_loc, s_met

    def finish(item, s_loc, s_met):
        j, gp, hl = item
        h = gp // 2
        c0 = j * BLOCK
        ha = 4 * gp + hl
        hb = ha + 2
        s_loc = s_loc + (bias_s[first_sel] if j == 0 else bias_s[0])
        sink = jnp.where(lane2 < BLOCK, sink_ref[ha], sink_ref[hb])
        mx = jnp.maximum(jnp.maximum(jnp.max(s_loc, axis=0, keepdims=True),
                                     jnp.max(s_met, axis=0, keepdims=True)), sink)
        p_loc = jnp.exp(s_loc - mx)
        p_met = jnp.exp(s_met - mx)
        den = (jnp.sum(p_loc, axis=0, keepdims=True)
               + jnp.sum(p_met, axis=0, keepdims=True) + jnp.exp(sink - mx))
        vT_h = vTbuf[h * HEAD_DIM:(h + 1) * HEAD_DIM, c0:c0 + 2 * BLOCK]
        vmT_h = vmetaT[h * HEAD_DIM:(h + 1) * HEAD_DIM, :]
        oT = _dot(vT_h, p_loc.astype(BF16)) + _dot(vmT_h, p_met.astype(BF16))
        oT = oT * (1.0 / den)
        oT_s[j, ha * HEAD_DIM:(ha + 1) * HEAD_DIM, :] = oT[:, :BLOCK]
        oT_s[j, hb * HEAD_DIM:(hb + 1) * HEAD_DIM, :] = oT[:, BLOCK:]

    pending = [scores(it) for it in items[:ATTN_LOOKAHEAD]]
    for i, it in enumerate(items):
        if i + ATTN_LOOKAHEAD < len(items):
            pending.append(scores(items[i + ATTN_LOOKAHEAD]))
        finish(it, *pending.pop(0))
    for j in range(nblk):
        o_blk = oT_s[j].T.astype(BF16)
        attn_s[j * BLOCK:(j + 1) * BLOCK, :] = _dot(o_blk, wap_ref[...])

    kbuf[:, 0:BLOCK, :] = kbuf[:, T:T + BLOCK, :]
    vTbuf[:, 0:BLOCK] = vTbuf[:, T:T + BLOCK]

    glu_in = _dot(hn, wglu_ref[...]) + bglu_ref[...]
    gbuf[CONV_HALO:CONV_HALO + T, :] = glu_in[:, :CONV_CH] * jax.nn.sigmoid(glu_in[:, CONV_CH:])
    for s in range(1, CONV_SHIFTS):
        shbuf[s - 1] = gbuf[SUBLANES - s:SUBLANES - s + T + CONV_PAD, :]

    nsub = CONV_ROWS // SUBLANES
    for cc in range(CONV_CH // CONV_COLS):
        cols = slice(cc * CONV_COLS, (cc + 1) * CONV_COLS)
        for rc in range(T // CONV_ROWS):
            r0 = rc * CONV_ROWS
            acc = jnp.broadcast_to(dwb_ref[:, cols].reshape(1, 1, CONV_COLS),
                                   (nsub, SUBLANES, CONV_COLS))
            for d in range(CONV_K):
                a, s = divmod(d, SUBLANES)
                w = dww_ref[CONV_K - 1 - d, :, cols]
                if s == 0:
                    g = gbuf[r0 + CONV_HALO - SUBLANES * a:r0 + CONV_HALO - SUBLANES * a + CONV_ROWS, cols]
                else:
                    g = shbuf[s - 1, r0 + CONV_PAD - SUBLANES * a:r0 + CONV_PAD - SUBLANES * a + CONV_ROWS, cols]
                acc = acc + g.reshape(nsub, SUBLANES, CONV_COLS) * w[None]
            cbuf[r0:r0 + CONV_ROWS, cols] = acc.reshape(CONV_ROWS, CONV_COLS)
    gbuf[0:CONV_HALO, :] = gbuf[T:T + CONV_HALO, :]

    c = _silu(_layer_norm(cbuf[...], lng_ref[...], lnb_ref[...]))
    conv = _dot(c.astype(BF16), wcp_ref[...]) + bcp_ref[...]

    gates = jax.nn.sigmoid(_dot(hn, wgate_ref[...]) + bgate_ref[...])
    merged = gates[:, :D_MODEL] * attn_s[...] + gates[:, D_MODEL:] * conv
    mix = _dot(merged.astype(BF16), wout_ref[...])
    out_ref[0] = x + _rms(mix, gpost_ref[...])


def _ffn_kernel(h_ref, gpre_ref, gpost_ref, wup_ref, dwp_ref, wdown_ref, umeta_ref,
                out_ref, hn_s, ubuf, carry, act_s, acc_s):
    T = FFN_TILE
    t = pl.program_id(1)

    @pl.when(t == 0)
    def _init():
        carry[...] = umeta_ref[...]

    h = h_ref[0]
    hn_s[...] = _rms(h, gpre_ref[...]).astype(BF16)

    total = None
    for cidx in range(N_FFN_CHUNKS):
        par = cidx % 2
        ys = []
        for half in range(2):
            idx = cidx + N_FFN_CHUNKS * half
            u = _dot(hn_s[...], wup_ref[idx])
            ubuf[par, half, 0:SUBLANES, :] = carry[idx]
            ubuf[par, half, SUBLANES:SUBLANES + T, :] = u
            carry[idx] = u[T - SUBLANES:T, :]
            w = dwp_ref[idx]
            y = (w[3:4, :] + w[0:1, :] * ubuf[par, half, SUBLANES - 2:SUBLANES - 2 + T, :]
                 + w[1:2, :] * ubuf[par, half, SUBLANES - 1:SUBLANES - 1 + T, :]
                 + w[2:3, :] * u)
            ys.append(y)
        act_s[:, cidx * FFN_CHUNK:(cidx + 1) * FFN_CHUNK] = (_silu(ys[0]) * ys[1]).astype(BF16)
        for lo, hi in FFN_DOWN_GROUPS:
            if cidx == hi - 1:
                part = _dot(act_s[:, lo * FFN_CHUNK:hi * FFN_CHUNK],
                            wdown_ref[lo * FFN_CHUNK:hi * FFN_CHUNK, :])
                if hi == N_FFN_CHUNKS:
                    total = acc_s[...] + part
                elif lo == 0:
                    acc_s[...] = part
                else:
                    acc_s[...] += part
    out_ref[0] = h + _rms(total, gpost_ref[...])


def _const_spec(shape):
    nd = len(shape)
    return pl.BlockSpec(shape, lambda *_: (0,) * nd, pipeline_mode=pl.Buffered(1))


def _rope_tables(n_pos):
    inv_freq = ROPE_THETA ** (-jnp.arange(ROT_HALF, dtype=F32) * 2.0 / ROT_DIM)
    ang = jnp.arange(n_pos).astype(F32)[:, None] * inv_freq[None, :]
    cos, sin = jnp.cos(ang), jnp.sin(ang)
    ones = jnp.ones((n_pos, HEAD_DIM - ROT_DIM), F32)
    zeros = jnp.zeros((n_pos, HEAD_DIM - ROT_HALF), F32)
    c = jnp.concatenate([cos, cos, ones], axis=1)
    s1 = jnp.concatenate([-sin, zeros], axis=1)
    s2 = jnp.concatenate([zeros[:, :ROT_HALF], sin, zeros[:, :HEAD_DIM - ROT_DIM]], axis=1)
    rep = LANES // HEAD_DIM
    return (jnp.tile(c, (1, rep)), jnp.tile(s1, (1, rep)), jnp.tile(s2, (1, rep)),
            cos.T, sin.T)


def kernel(x, meta_tokens, norm_pre_mix, norm_post_mix, w_in, b_in, attn_sinks, w_attn_proj,
           conv_dw_w, conv_dw_b, conv_ln_g, conv_ln_b, w_conv_proj, b_conv_proj, w_out,
           norm_pre_ffn, norm_post_ffn, w_up, ffn_dw_w, ffn_dw_b, w_down):
    bsz, seq, _ = x.shape
    assert seq % MIX_TILE == 0 and seq % FFN_TILE == 0
    scale = HEAD_DIM ** -0.5
    row = lambda v: v.reshape(1, -1).astype(F32)

    wi, bi = w_in[0], b_in[0]
    c_q, c_k, c_v, c_glu = ATTN_WIDTH, ATTN_WIDTH + KV_WIDTH, ATTN_WIDTH + 2 * KV_WIDTH, \
        ATTN_WIDTH + 2 * KV_WIDTH + 2 * CONV_CH
    wqT = (wi[:, :c_q] * scale).T.astype(BF16)
    bq = bi[:c_q] * scale
    wk = wi[:, c_q:c_k].astype(BF16)
    bk = row(bi[c_q:c_k])
    wvT = wi[:, c_k:c_v].T.astype(BF16)
    bv = bi[c_k:c_v]
    wglu = wi[:, c_v:c_glu].astype(BF16)
    bglu = row(bi[c_v:c_glu])
    wgate = wi[:, c_glu:].astype(BF16)
    bgate = row(bi[c_glu:])
    sinks = attn_sinks[0].astype(F32)
    wap = w_attn_proj[0].astype(BF16)
    dww = conv_dw_w[0].astype(F32)
    dww8 = jnp.broadcast_to(dww[:, None, :], (CONV_K, SUBLANES, CONV_CH))
    dwb = row(conv_dw_b[0])
    lng, lnb = row(conv_ln_g[0]), row(conv_ln_b[0])
    wcp = w_conv_proj[0].astype(BF16)
    bcp = row(b_conv_proj[0])
    wout = w_out[0].astype(BF16)
    gpre, gpost = row(norm_pre_mix[0]), row(norm_post_mix[0])
    gffn, gffn_post = row(norm_pre_ffn[0]), row(norm_post_ffn[0])

    nch = 2 * N_FFN_CHUNKS
    wup = w_up[0].reshape(D_MODEL, nch, FFN_CHUNK).transpose(1, 0, 2).astype(BF16)
    dwp = jnp.concatenate(
        [ffn_dw_w[0], ffn_dw_b[0][None, :], jnp.zeros((SUBLANES - FFN_CONV_K - 1, 2 * FFN_DIM), F32)],
        axis=0).reshape(SUBLANES, nch, FFN_CHUNK).transpose(1, 0, 2)
    wdown = w_down[0].astype(BF16)

    rc, rs1, rs2, cosT, sinT = _rope_tables(N_META + seq)

    smem = pl.BlockSpec(memory_space=pltpu.SMEM)
    vmem = pl.BlockSpec(memory_space=pltpu.VMEM)

    k_meta, v_meta, glu_meta, u_meta = pl.pallas_call(
        _meta_kernel,
        out_shape=(jax.ShapeDtypeStruct((N_META, KV_WIDTH), F32),
                   jax.ShapeDtypeStruct((N_META, KV_WIDTH), F32),
                   jax.ShapeDtypeStruct((N_META, CONV_CH), F32),
                   jax.ShapeDtypeStruct((nch, N_META, FFN_CHUNK), F32)),
        in_specs=[vmem] * 12 + [smem] + [vmem] * 14,
        out_specs=(vmem, vmem, vmem, vmem),
        scratch_shapes=[pltpu.VMEM((CONV_HALO + N_META, CONV_CH), F32)],
        compiler_params=pltpu.CompilerParams(vmem_limit_bytes=VMEM_LIMIT),
        name="meta_prologue",
    )(meta_tokens.astype(F32), gpre, wqT, row(bq), wk, bk, wvT, row(bv), wglu, bglu, wgate,
      bgate, sinks, wap, dww, dwb, lng, lnb, wcp, bcp, wout, gpost, gffn, wup,
      rc[:N_META], rs1[:N_META], rs2[:N_META])

    vmetaT = v_meta.T.astype(BF16)
    umeta_tail = u_meta[:, N_META - SUBLANES:, :]

    T = MIX_TILE
    bqT = jnp.broadcast_to(bq[:, None], (ATTN_WIDTH, T)).astype(F32)
    bvT = jnp.broadcast_to(bv[:, None], (KV_WIDTH, T)).astype(F32)
    tile_spec = pl.BlockSpec((1, T, D_MODEL), lambda b, t: (b, t, 0))
    rope_spec = pl.BlockSpec((T, LANES), lambda b, t: (t, 0))
    ropeT_spec = pl.BlockSpec((SUBLANES, T), lambda b, t: (0, t))
    mixer_in = [
        (x, tile_spec), (gpre, None), (wqT, None), (bqT, None), (wk, None), (bk, None),
        (wvT, None), (bvT, None), (wglu, None), (bglu, None), (wgate, None), (bgate, None),
        (sinks, smem), (wap, None), (dww8, None), (dwb, None), (lng, None), (lnb, None),
        (wcp, None), (bcp, None), (wout, None), (gpost, None),
        (rc[N_META:], rope_spec), (rs1[N_META:], rope_spec), (rs2[N_META:], rope_spec),
        (cosT[:, N_META:], ropeT_spec), (sinT[:, N_META:], ropeT_spec),
        (k_meta, None), (vmetaT, None), (glu_meta, None),
    ]
    h2 = pl.pallas_call(
        _mixer_kernel,
        out_shape=jax.ShapeDtypeStruct((bsz, seq, D_MODEL), F32),
        grid=(bsz, seq // T),
        in_specs=[spec if spec is not None else _const_spec(a.shape) for a, spec in mixer_in],
        out_specs=tile_spec,
        scratch_shapes=[
            pltpu.VMEM((4, BLOCK + T, LANES), BF16),
            pltpu.VMEM((KV_WIDTH, BLOCK + T), BF16),
            pltpu.VMEM((CONV_HALO + T, CONV_CH), F32),
            pltpu.VMEM((CONV_SHIFTS - 1, T + CONV_PAD, CONV_CH), F32),
            pltpu.VMEM((ATTN_WIDTH, T), BF16),
            pltpu.VMEM((T // BLOCK, ATTN_WIDTH, BLOCK), F32),
            pltpu.VMEM((T, D_MODEL), F32),
            pltpu.VMEM((T, CONV_CH), F32),
            pltpu.VMEM((2, 2 * BLOCK, 2 * BLOCK), F32),
        ],
        compiler_params=pltpu.CompilerParams(
            dimension_semantics=("arbitrary", "arbitrary"), vmem_limit_bytes=VMEM_LIMIT),
        name="mixer",
    )(*[a for a, _ in mixer_in])

    T2 = FFN_TILE
    tile2 = pl.BlockSpec((1, T2, D_MODEL), lambda b, t: (b, t, 0))
    ffn_in = [(h2, tile2), (gffn, None), (gffn_post, None), (wup, None), (dwp, None),
              (wdown, None), (umeta_tail, None)]
    out = pl.pallas_call(
        _ffn_kernel,
        out_shape=jax.ShapeDtypeStruct((bsz, seq, D_MODEL), F32),
        grid=(bsz, seq // T2),
        in_specs=[spec if spec is not None else _const_spec(a.shape) for a, spec in ffn_in],
        out_specs=tile2,
        scratch_shapes=[
            pltpu.VMEM((T2, D_MODEL), BF16),
            pltpu.VMEM((2, 2, SUBLANES + T2, FFN_CHUNK), F32),
            pltpu.VMEM((nch, SUBLANES, FFN_CHUNK), F32),
            pltpu.VMEM((T2, FFN_DIM), BF16),
            pltpu.VMEM((T2, D_MODEL), F32),
        ],
        compiler_params=pltpu.CompilerParams(
            dimension_semantics=("arbitrary", "arbitrary"), vmem_limit_bytes=VMEM_LIMIT),
        name="conv_ffn",
    )(*[a for a, _ in ffn_in])
    return out.astype(x.dtype)
```

```python
import functools

import numpy as np
import jax
import jax.numpy as jnp
from jax import lax
from jax.experimental import pallas as pl
from jax.experimental.pallas import tpu as pltpu

D_MODEL = 1024
N_META = 16
N_Q_HEADS = 16
N_KV_HEADS = 2
HEAD_DIM = 64
GROUP = N_Q_HEADS // N_KV_HEADS
ROT_DIM = HEAD_DIM // 4
ROT_HALF = ROT_DIM // 2
ROPE_THETA = 500000.0
WINDOW = 128
BLOCK = 128
ATTN_WIDTH = N_Q_HEADS * HEAD_DIM
KV_WIDTH = N_KV_HEADS * HEAD_DIM
CONV_CH = D_MODEL
CONV_K = 31
FFN_DIM = 2816
FFN_CONV_K = 3
RMS_EPS = 1e-6
LN_EPS = 1e-5
NEG_INF = -1e30

LANES = 128
SUBLANES = 8
MIX_TILE = 256
FFN_TILE = 512
FFN_CHUNK = 256
N_FFN_CHUNKS = FFN_DIM // FFN_CHUNK
FFN_DOWN_GROUPS = ((0, 3), (3, 6), (6, 9), (9, 11))
CONV_HALO = 32
CONV_SHIFTS = SUBLANES
CONV_PAD = CONV_HALO - SUBLANES
CONV_ROWS = 32
CONV_COLS = 256
ATTN_LOOKAHEAD = 2
VMEM_LIMIT = 56 * 1024 * 1024

F32 = jnp.float32
BF16 = jnp.bfloat16

_NT = (((1,), (1,)), ((), ()))


def _dot(a, b):
    return jnp.dot(a, b, preferred_element_type=F32)


def _dot_nt(a, b):
    return lax.dot_general(a, b, _NT, preferred_element_type=F32)


def _rms(x, g):
    ms = jnp.mean(x * x, axis=-1, keepdims=True)
    return x * lax.rsqrt(ms + RMS_EPS) * g


def _layer_norm(x, g, b):
    mu = jnp.mean(x, axis=-1, keepdims=True)
    xc = x - mu
    var = jnp.mean(xc * xc, axis=-1, keepdims=True)
    return xc * lax.rsqrt(var + LN_EPS) * g + b


def _sigmoid(x):
    return 0.5 * jnp.tanh(0.5 * x) + 0.5


def _silu(x):
    hx = 0.5 * x
    return hx * jnp.tanh(hx) + hx


def _rope_rows(z, c, s1, s2):
    parts = []
    for g in range(z.shape[1] // LANES):
        zg = z[:, g * LANES:(g + 1) * LANES]
        parts.append(zg * c + pltpu.roll(zg, LANES - ROT_HALF, 1) * s1
                     + pltpu.roll(zg, ROT_HALF, 1) * s2)
    return parts[0] if len(parts) == 1 else jnp.concatenate(parts, axis=1)


def _key_variants(k):
    lane = lax.broadcasted_iota(jnp.int32, k.shape, 1)
    lo = lane < HEAD_DIM
    kr = pltpu.roll(k, HEAD_DIM, 1)
    zero = jnp.zeros_like(k)
    return [jnp.where(lo, k, zero).astype(BF16), jnp.where(lo, zero, kr).astype(BF16),
            jnp.where(lo, kr, zero).astype(BF16), jnp.where(lo, zero, k).astype(BF16)]


def _meta_kernel(meta_ref, gpre_ref, wqT_ref, bq_ref, wk_ref, bk_ref, wvT_ref, bv_ref,
                 wglu_ref, bglu_ref, wgate_ref, bgate_ref, sink_ref, wap_ref, dww_ref,
                 dwb_ref, lng_ref, lnb_ref, wcp_ref, bcp_ref, wout_ref, gpost_ref,
                 gffn_ref, wup_ref, rc_ref, rs1_ref, rs2_ref,
                 k_out, v_out, glu_out, u_out, gscr):
    m = meta_ref[...]
    hn = _rms(m, gpre_ref[...]).astype(BF16)
    q = _dot_nt(hn, wqT_ref[...]) + bq_ref[...]
    k = _dot(hn, wk_ref[...]) + bk_ref[...]
    v = _dot_nt(hn, wvT_ref[...]) + bv_ref[...]
    rc, rs1, rs2 = rc_ref[...], rs1_ref[...], rs2_ref[...]
    q = _rope_rows(q, rc, rs1, rs2)
    k = _rope_rows(k, rc, rs1, rs2)
    k_out[...] = k
    v_out[...] = v

    row = lax.broadcasted_iota(jnp.int32, (N_META, N_META), 0)
    col = lax.broadcasted_iota(jnp.int32, (N_META, N_META), 1)
    causal = col <= row
    qb, kb, vb = q.astype(BF16), k.astype(BF16), v.astype(BF16)
    attn = jnp.zeros((N_META, D_MODEL), F32)
    for hq in range(N_Q_HEADS):
        h = hq // GROUP
        qh = qb[:, hq * HEAD_DIM:(hq + 1) * HEAD_DIM]
        kh = kb[:, h * HEAD_DIM:(h + 1) * HEAD_DIM]
        vh = vb[:, h * HEAD_DIM:(h + 1) * HEAD_DIM]
        s = jnp.where(causal, _dot_nt(qh, kh), NEG_INF)
        sink = sink_ref[hq]
        mx = jnp.maximum(jnp.max(s, axis=-1, keepdims=True), sink)
        p = jnp.exp(s - mx)
        den = jnp.sum(p, axis=-1, keepdims=True) + jnp.exp(sink - mx)
        o = _dot(p.astype(BF16), vh) / den
        attn = attn + _dot(o.astype(BF16), wap_ref[hq * HEAD_DIM:(hq + 1) * HEAD_DIM, :])

    glu_in = _dot(hn, wglu_ref[...]) + bglu_ref[...]
    glu = glu_in[:, :CONV_CH] * _sigmoid(glu_in[:, CONV_CH:])
    glu_out[...] = glu
    gscr[0:CONV_HALO, :] = jnp.zeros((CONV_HALO, CONV_CH), F32)
    gscr[CONV_HALO:CONV_HALO + N_META, :] = glu
    c = jnp.broadcast_to(dwb_ref[...], (N_META, CONV_CH))
    off = CONV_HALO - (CONV_K - 1)
    for kk in range(CONV_K):
        c = c + gscr[off + kk:off + kk + N_META, :] * dww_ref[kk:kk + 1, :]
    c = _silu(_layer_norm(c, lng_ref[...], lnb_ref[...]))
    conv = _dot(c.astype(BF16), wcp_ref[...]) + bcp_ref[...]

    gates = _sigmoid(_dot(hn, wgate_ref[...]) + bgate_ref[...])
    merged = gates[:, :D_MODEL] * attn + gates[:, D_MODEL:] * conv
    mix = _dot(merged.astype(BF16), wout_ref[...])
    h2 = m + _rms(mix, gpost_ref[...])
    hn2 = _rms(h2, gffn_ref[...]).astype(BF16)
    for cidx in range(2 * N_FFN_CHUNKS):
        u_out[cidx] = _dot(hn2, wup_ref[cidx])


def _mixer_kernel(x_ref, gpre_ref, wqT_ref, bqT_ref, wk_ref, bk_ref, wvT_ref, bvT_ref,
                  wglu_ref, bglu_ref, wgate_ref, bgate_ref, sink_ref, wap_ref, dww_ref,
                  dwb_ref, lng_ref, lnb_ref, wcp_ref, bcp_ref, wout_ref, gpost_ref,
                  rc_ref, rs1_ref, rs2_ref, cosT_ref, sinT_ref,
                  kmeta_ref, vmetaT_ref, glumeta_ref,
                  out_ref,
                  kbuf, vTbuf, gbuf, shbuf, qT_s, oT_s, attn_s, cbuf, bias_s):
    T = MIX_TILE
    nblk = T // BLOCK
    t = pl.program_id(1)

    @pl.when(t == 0)
    def _init():
        kbuf[:, 0:BLOCK, :] = jnp.zeros((4, BLOCK, LANES), BF16)
        vTbuf[:, 0:BLOCK] = jnp.zeros((KV_WIDTH, BLOCK), BF16)
        gbuf[0:CONV_HALO - N_META, :] = jnp.zeros((CONV_HALO - N_META, CONV_CH), F32)
        gbuf[CONV_HALO - N_META:CONV_HALO, :] = glumeta_ref[...]
        key = lax.broadcasted_iota(jnp.int32, (2 * BLOCK, 2 * BLOCK), 0)
        qry = lax.broadcasted_iota(jnp.int32, (2 * BLOCK, 2 * BLOCK), 1) % BLOCK
        vis = (key > qry) & (key <= qry + WINDOW)
        bias_s[0] = jnp.where(vis, 0.0, NEG_INF).astype(F32)
        bias_s[1] = jnp.where(vis & (key >= BLOCK), 0.0, NEG_INF).astype(F32)

    x = x_ref[0]
    hn = _rms(x, gpre_ref[...]).astype(BF16)

    glu_in = _dot(hn, wglu_ref[...]) + bglu_ref[...]
    gbuf[CONV_HALO:CONV_HALO + T, :] = glu_in[:, :CONV_CH] * _sigmoid(glu_in[:, CONV_CH:])
    for s in range(1, CONV_SHIFTS):
        shbuf[s - 1] = gbuf[SUBLANES - s:SUBLANES - s + T + CONV_PAD, :]

    qT = _dot_nt(wqT_ref[...], hn) + bqT_ref[...]
    cosT, sinT = cosT_ref[...], sinT_ref[...]
    for hq in range(N_Q_HEADS):
        base = hq * HEAD_DIM
        r0 = qT[base:base + ROT_HALF]
        r1 = qT[base + ROT_HALF:base + ROT_DIM]
        rot = jnp.concatenate([r0 * cosT - r1 * sinT, r1 * cosT + r0 * sinT], axis=0)
        qT_s[base:base + ROT_DIM, :] = rot.astype(BF16)
        qT_s[base + ROT_DIM:base + HEAD_DIM, :] = qT[base + ROT_DIM:base + HEAD_DIM].astype(BF16)

    k = _dot(hn, wk_ref[...]) + bk_ref[...]
    k = _rope_rows(k, rc_ref[...], rs1_ref[...], rs2_ref[...])
    for i, kv in enumerate(_key_variants(k)):
        kbuf[i, BLOCK:BLOCK + T, :] = kv
    kmeta_var = _key_variants(kmeta_ref[...])
    vT = _dot_nt(wvT_ref[...], hn) + bvT_ref[...]
    vTbuf[:, BLOCK:BLOCK + T] = vT.astype(BF16)
    vmetaT = vmetaT_ref[...]

    lane2 = lax.broadcasted_iota(jnp.int32, (1, 2 * BLOCK), 1)
    first_sel = jnp.where(t == 0, 1, 0)

    items = [(j, gp, hl) for j in range(nblk) for gp in range(4) for hl in range(2)]

    def scores(item):
        j, gp, hl = item
        h = gp // 2
        c0 = j * BLOCK
        rhs = jnp.concatenate(
            [qT_s[256 * gp:256 * gp + 128, c0:c0 + BLOCK],
             qT_s[256 * gp + 128:256 * gp + 256, c0:c0 + BLOCK]], axis=1)
        s_loc = _dot(kbuf[2 * h + hl, c0:c0 + 2 * BLOCK, :], rhs)
        s_met = _dot(kmeta_var[2 * h + hl], rhs)
        return s_loc, s_met

    def finish(item, s_loc, s_met):
        j, gp, hl = item
        h = gp // 2
        c0 = j * BLOCK
        ha = 4 * gp + hl
        hb = ha + 2
        s_loc = s_loc + (bias_s[first_sel] if j == 0 else bias_s[0])
        sink = jnp.where(lane2 < BLOCK, sink_ref[ha], sink_ref[hb])
        mx = jnp.maximum(jnp.maximum(jnp.max(s_loc, axis=0, keepdims=True),
                                     jnp.max(s_met, axis=0, keepdims=True)), sink)
        p_loc = jnp.exp(s_loc - mx)
        p_met = jnp.exp(s_met - mx)
        den = (jnp.sum(p_loc, axis=0, keepdims=True)
               + jnp.sum(p_met, axis=0, keepdims=True) + jnp.exp(sink - mx))
        vT_h = vTbuf[h * HEAD_DIM:(h + 1) * HEAD_DIM, c0:c0 + 2 * BLOCK]
        vmT_h = vmetaT[h * HEAD_DIM:(h + 1) * HEAD_DIM, :]
        oT = _dot(vT_h, p_loc.astype(BF16)) + _dot(vmT_h, p_met.astype(BF16))
        oT = oT * (1.0 / den)
        oT_s[j, ha * HEAD_DIM:(ha + 1) * HEAD_DIM, :] = oT[:, :BLOCK]
        oT_s[j, hb * HEAD_DIM:(hb + 1) * HEAD_DIM, :] = oT[:, BLOCK:]

    def conv_block(cc, rc):
        cols = slice(cc * CONV_COLS, (cc + 1) * CONV_COLS)
        r0 = rc * CONV_ROWS
        nsub = CONV_ROWS // SUBLANES
        bias8 = jnp.broadcast_to(dwb_ref[:, cols], (SUBLANES, CONV_COLS))
        accs = [bias8] * nsub
        for d in range(CONV_K):
            a, s = divmod(d, SUBLANES)
            w = dww_ref[CONV_K - 1 - d, :, cols]
            for r in range(nsub):
                if s == 0:
                    lo = r0 + r * SUBLANES + CONV_HALO - SUBLANES * a
                    g = gbuf[lo:lo + SUBLANES, cols]
                else:
                    lo = r0 + r * SUBLANES + CONV_PAD - SUBLANES * a
                    g = shbuf[s - 1, lo:lo + SUBLANES, cols]
                accs[r] = accs[r] + g * w
        for r in range(nsub):
            cbuf[r0 + r * SUBLANES:r0 + (r + 1) * SUBLANES, cols] = accs[r]

    conv_blocks = [(cc, rc) for cc in range(CONV_CH // CONV_COLS) for rc in range(T // CONV_ROWS)]
    conv_per_item = -(-len(conv_blocks) // len(items))

    pending = [scores(it) for it in items[:ATTN_LOOKAHEAD]]
    gate_parts = []
    for i, it in enumerate(items):
        if i + ATTN_LOOKAHEAD < len(items):
            pending.append(scores(items[i + ATTN_LOOKAHEAD]))
        finish(it, *pending.pop(0))
        for _ in range(conv_per_item):
            if conv_blocks:
                conv_block(*conv_blocks.pop(0))
        if (i + 1) % (len(items) // nblk) == 0:
            j = i // (len(items) // nblk)
            o_blk = oT_s[j].T.astype(BF16)
            attn_s[j * BLOCK:(j + 1) * BLOCK, :] = _dot(o_blk, wap_ref[...])
            half = slice(j * (2 * D_MODEL // nblk), (j + 1) * (2 * D_MODEL // nblk))
            gate_parts.append(_sigmoid(_dot(hn, wgate_ref[:, half]) + bgate_ref[:, half]))
    while conv_blocks:
        conv_block(*conv_blocks.pop(0))
    gates = jnp.concatenate(gate_parts, axis=1)

    kbuf[:, 0:BLOCK, :] = kbuf[:, T:T + BLOCK, :]
    vTbuf[:, 0:BLOCK] = vTbuf[:, T:T + BLOCK]
    gbuf[0:CONV_HALO, :] = gbuf[T:T + CONV_HALO, :]

    c = _silu(_layer_norm(cbuf[...], lng_ref[...], lnb_ref[...]))
    conv = _dot(c.astype(BF16), wcp_ref[...]) + bcp_ref[...]

    merged = gates[:, :D_MODEL] * attn_s[...] + gates[:, D_MODEL:] * conv
    mix = _dot(merged.astype(BF16), wout_ref[...])
    out_ref[0] = x + _rms(mix, gpost_ref[...])


def _ffn_kernel(h_ref, gpre_ref, gpost_ref, wup_ref, dwp_ref, wdown_ref, umeta_ref,
                out_ref, hn_s, ubuf, carry, act_s, acc_s):
    T = FFN_TILE
    t = pl.program_id(1)

    @pl.when(t == 0)
    def _init():
        carry[...] = umeta_ref[...]

    h = h_ref[0]
    hn_s[...] = _rms(h, gpre_ref[...]).astype(BF16)

    def up(cidx):
        par = cidx % 2
        for half in range(2):
            idx = cidx + N_FFN_CHUNKS * half
            u = _dot(hn_s[...], wup_ref[idx])
            ubuf[par, half, 0:SUBLANES, :] = carry[idx]
            ubuf[par, half, SUBLANES:SUBLANES + T, :] = u
            carry[idx] = u[T - SUBLANES:T, :]

    def activate(cidx):
        par = cidx % 2
        ys = []
        for half in range(2):
            w = dwp_ref[cidx + N_FFN_CHUNKS * half]
            ys.append(w[3:4, :]
                      + w[0:1, :] * ubuf[par, half, SUBLANES - 2:SUBLANES - 2 + T, :]
                      + w[1:2, :] * ubuf[par, half, SUBLANES - 1:SUBLANES - 1 + T, :]
                      + w[2:3, :] * ubuf[par, half, SUBLANES:SUBLANES + T, :])
        act_s[:, cidx * FFN_CHUNK:(cidx + 1) * FFN_CHUNK] = (_silu(ys[0]) * ys[1]).astype(BF16)

    total = None
    up(0)
    for cidx in range(N_FFN_CHUNKS):
        if cidx + 1 < N_FFN_CHUNKS:
            up(cidx + 1)
        activate(cidx)
        for lo, hi in FFN_DOWN_GROUPS:
            if cidx == hi - 1:
                part = _dot(act_s[:, lo * FFN_CHUNK:hi * FFN_CHUNK],
                            wdown_ref[lo * FFN_CHUNK:hi * FFN_CHUNK, :])
                if hi == N_FFN_CHUNKS:
                    total = acc_s[...] + part
                elif lo == 0:
                    acc_s[...] = part
                else:
                    acc_s[...] += part
    out_ref[0] = h + _rms(total, gpost_ref[...])


def _const_spec(shape):
    nd = len(shape)
    return pl.BlockSpec(shape, lambda *_: (0,) * nd, pipeline_mode=pl.Buffered(1))


def _rope_tables(n_pos):
    inv_freq = ROPE_THETA ** (-jnp.arange(ROT_HALF, dtype=F32) * 2.0 / ROT_DIM)
    ang = jnp.arange(n_pos).astype(F32)[:, None] * inv_freq[None, :]
    cos, sin = jnp.cos(ang), jnp.sin(ang)
    ones = jnp.ones((n_pos, HEAD_DIM - ROT_DIM), F32)
    zeros = jnp.zeros((n_pos, HEAD_DIM - ROT_HALF), F32)
    c = jnp.concatenate([cos, cos, ones], axis=1)
    s1 = jnp.concatenate([-sin, zeros], axis=1)
    s2 = jnp.concatenate([zeros[:, :ROT_HALF], sin, zeros[:, :HEAD_DIM - ROT_DIM]], axis=1)
    rep = LANES // HEAD_DIM
    return (jnp.tile(c, (1, rep)), jnp.tile(s1, (1, rep)), jnp.tile(s2, (1, rep)),
            cos.T, sin.T)


def kernel(x, meta_tokens, norm_pre_mix, norm_post_mix, w_in, b_in, attn_sinks, w_attn_proj,
           conv_dw_w, conv_dw_b, conv_ln_g, conv_ln_b, w_conv_proj, b_conv_proj, w_out,
           norm_pre_ffn, norm_post_ffn, w_up, ffn_dw_w, ffn_dw_b, w_down):
    bsz, seq, _ = x.shape
    assert seq % MIX_TILE == 0 and seq % FFN_TILE == 0
    scale = HEAD_DIM ** -0.5
    row = lambda v: v.reshape(1, -1).astype(F32)

    wi, bi = w_in[0], b_in[0]
    c_q, c_k, c_v, c_glu = ATTN_WIDTH, ATTN_WIDTH + KV_WIDTH, ATTN_WIDTH + 2 * KV_WIDTH, \
        ATTN_WIDTH + 2 * KV_WIDTH + 2 * CONV_CH
    wqT = (wi[:, :c_q] * scale).T.astype(BF16)
    bq = bi[:c_q] * scale
    wk = wi[:, c_q:c_k].astype(BF16)
    bk = row(bi[c_q:c_k])
    wvT = wi[:, c_k:c_v].T.astype(BF16)
    bv = bi[c_k:c_v]
    wglu = wi[:, c_v:c_glu].astype(BF16)
    bglu = row(bi[c_v:c_glu])
    wgate = wi[:, c_glu:].astype(BF16)
    bgate = row(bi[c_glu:])
    sinks = attn_sinks[0].astype(F32)
    wap = w_attn_proj[0].astype(BF16)
    dww = conv_dw_w[0].astype(F32)
    dww8 = jnp.broadcast_to(dww[:, None, :], (CONV_K, SUBLANES, CONV_CH))
    dwb = row(conv_dw_b[0])
    lng, lnb = row(conv_ln_g[0]), row(conv_ln_b[0])
    wcp = w_conv_proj[0].astype(BF16)
    bcp = row(b_conv_proj[0])
    wout = w_out[0].astype(BF16)
    gpre, gpost = row(norm_pre_mix[0]), row(norm_post_mix[0])
    gffn, gffn_post = row(norm_pre_ffn[0]), row(norm_post_ffn[0])

    nch = 2 * N_FFN_CHUNKS
    wup = w_up[0].reshape(D_MODEL, nch, FFN_CHUNK).transpose(1, 0, 2).astype(BF16)
    dwp = jnp.concatenate(
        [ffn_dw_w[0], ffn_dw_b[0][None, :], jnp.zeros((SUBLANES - FFN_CONV_K - 1, 2 * FFN_DIM), F32)],
        axis=0).reshape(SUBLANES, nch, FFN_CHUNK).transpose(1, 0, 2)
    wdown = w_down[0].astype(BF16)

    rc, rs1, rs2, cosT, sinT = _rope_tables(N_META + seq)

    smem = pl.BlockSpec(memory_space=pltpu.SMEM)
    vmem = pl.BlockSpec(memory_space=pltpu.VMEM)

    k_meta, v_meta, glu_meta, u_meta = pl.pallas_call(
        _meta_kernel,
        out_shape=(jax.ShapeDtypeStruct((N_META, KV_WIDTH), F32),
                   jax.ShapeDtypeStruct((N_META, KV_WIDTH), F32),
                   jax.ShapeDtypeStruct((N_META, CONV_CH), F32),
                   jax.ShapeDtypeStruct((nch, N_META, FFN_CHUNK), F32)),
        in_specs=[vmem] * 12 + [smem] + [vmem] * 14,
        out_specs=(vmem, vmem, vmem, vmem),
        scratch_shapes=[pltpu.VMEM((CONV_HALO + N_META, CONV_CH), F32)],
        compiler_params=pltpu.CompilerParams(vmem_limit_bytes=VMEM_LIMIT),
        name="meta_prologue",
    )(meta_tokens.astype(F32), gpre, wqT, row(bq), wk, bk, wvT, row(bv), wglu, bglu, wgate,
      bgate, sinks, wap, dww, dwb, lng, lnb, wcp, bcp, wout, gpost, gffn, wup,
      rc[:N_META], rs1[:N_META], rs2[:N_META])

    vmetaT = v_meta.T.astype(BF16)
    umeta_tail = u_meta[:, N_META - SUBLANES:, :]

    T = MIX_TILE
    bqT = jnp.broadcast_to(bq[:, None], (ATTN_WIDTH, T)).astype(F32)
    bvT = jnp.broadcast_to(bv[:, None], (KV_WIDTH, T)).astype(F32)
    tile_spec = pl.BlockSpec((1, T, D_MODEL), lambda b, t: (b, t, 0))
    rope_spec = pl.BlockSpec((T, LANES), lambda b, t: (t, 0))
    ropeT_spec = pl.BlockSpec((SUBLANES, T), lambda b, t: (0, t))
    mixer_in = [
        (x, tile_spec), (gpre, None), (wqT, None), (bqT, None), (wk, None), (bk, None),
        (wvT, None), (bvT, None), (wglu, None), (bglu, None), (wgate, None), (bgate, None),
        (sinks, smem), (wap, None), (dww8, None), (dwb, None), (lng, None), (lnb, None),
        (wcp, None), (bcp, None), (wout, None), (gpost, None),
        (rc[N_META:], rope_spec), (rs1[N_META:], rope_spec), (rs2[N_META:], rope_spec),
        (cosT[:, N_META:], ropeT_spec), (sinT[:, N_META:], ropeT_spec),
        (k_meta, None), (vmetaT, None), (glu_meta, None),
    ]
    h2 = pl.pallas_call(
        _mixer_kernel,
        out_shape=jax.ShapeDtypeStruct((bsz, seq, D_MODEL), F32),
        grid=(bsz, seq // T),
        in_specs=[spec if spec is not None else _const_spec(a.shape) for a, spec in mixer_in],
        out_specs=tile_spec,
        scratch_shapes=[
            pltpu.VMEM((4, BLOCK + T, LANES), BF16),
            pltpu.VMEM((KV_WIDTH, BLOCK + T), BF16),
            pltpu.VMEM((CONV_HALO + T, CONV_CH), F32),
            pltpu.VMEM((CONV_SHIFTS - 1, T + CONV_PAD, CONV_CH), F32),
            pltpu.VMEM((ATTN_WIDTH, T), BF16),
            pltpu.VMEM((T // BLOCK, ATTN_WIDTH, BLOCK), F32),
            pltpu.VMEM((T, D_MODEL), F32),
            pltpu.VMEM((T, CONV_CH), F32),
            pltpu.VMEM((2, 2 * BLOCK, 2 * BLOCK), F32),
        ],
        compiler_params=pltpu.CompilerParams(
            dimension_semantics=("arbitrary", "arbitrary"), vmem_limit_bytes=VMEM_LIMIT),
        name="mixer",
    )(*[a for a, _ in mixer_in])

    T2 = FFN_TILE
    tile2 = pl.BlockSpec((1, T2, D_MODEL), lambda b, t: (b, t, 0))
    ffn_in = [(h2, tile2), (gffn, None), (gffn_post, None), (wup, None), (dwp, None),
              (wdown, None), (umeta_tail, None)]
    out = pl.pallas_call(
        _ffn_kernel,
        out_shape=jax.ShapeDtypeStruct((bsz, seq, D_MODEL), F32),
        grid=(bsz, seq // T2),
        in_specs=[spec if spec is not None else _const_spec(a.shape) for a, spec in ffn_in],
        out_specs=tile2,
        scratch_shapes=[
            pltpu.VMEM((T2, D_MODEL), BF16),
            pltpu.VMEM((2, 2, SUBLANES + T2, FFN_CHUNK), F32),
            pltpu.VMEM((nch, SUBLANES, FFN_CHUNK), F32),
            pltpu.VMEM((T2, FFN_DIM), BF16),
            pltpu.VMEM((T2, D_MODEL), F32),
        ],
        compiler_params=pltpu.CompilerParams(
            dimension_semantics=("arbitrary", "arbitrary"), vmem_limit_bytes=VMEM_LIMIT),
        name="conv_ffn",
    )(*[a for a, _ in ffn_in])
    return out.astype(x.dtype)
```

```python
import functools

import numpy as np
import jax
import jax.numpy as jnp
from jax import lax
from jax.experimental import pallas as pl
from jax.experimental.pallas import tpu as pltpu

D_MODEL = 1024
N_META = 16
N_Q_HEADS = 16
N_KV_HEADS = 2
HEAD_DIM = 64
GROUP = N_Q_HEADS // N_KV_HEADS
ROT_DIM = HEAD_DIM // 4
ROT_HALF = ROT_DIM // 2
ROPE_THETA = 500000.0
WINDOW = 128
BLOCK = 128
ATTN_WIDTH = N_Q_HEADS * HEAD_DIM
KV_WIDTH = N_KV_HEADS * HEAD_DIM
CONV_CH = D_MODEL
CONV_K = 31
FFN_DIM = 2816
FFN_CONV_K = 3
RMS_EPS = 1e-6
LN_EPS = 1e-5
NEG_INF = -1e30

LANES = 128
SUBLANES = 8
MIX_TILE = 512
FFN_TILE = 512
FFN_CHUNK = 256
N_FFN_CHUNKS = FFN_DIM // FFN_CHUNK
FFN_DOWN_GROUPS = ((0, 3), (3, 6), (6, 9), (9, 11))
CONV_HALO = 32
CONV_SHIFTS = SUBLANES
CONV_PAD = CONV_HALO - SUBLANES
CONV_ROWS = 32
CONV_COLS = 256
ATTN_LOOKAHEAD = 2
VMEM_LIMIT = 56 * 1024 * 1024

F32 = jnp.float32
BF16 = jnp.bfloat16

_NT = (((1,), (1,)), ((), ()))


def _dot(a, b):
    return jnp.dot(a, b, preferred_element_type=F32)


def _dot_nt(a, b):
    return lax.dot_general(a, b, _NT, preferred_element_type=F32)


def _rms(x, g):
    ms = jnp.mean(x * x, axis=-1, keepdims=True)
    return x * lax.rsqrt(ms + RMS_EPS) * g


def _layer_norm(x, g, b):
    mu = jnp.mean(x, axis=-1, keepdims=True)
    xc = x - mu
    var = jnp.mean(xc * xc, axis=-1, keepdims=True)
    return xc * lax.rsqrt(var + LN_EPS) * g + b


def _sigmoid(x):
    return 0.5 * jnp.tanh(0.5 * x) + 0.5


def _silu(x):
    hx = 0.5 * x
    return hx * jnp.tanh(hx) + hx


def _rope_rows(z, c, s1, s2):
    parts = []
    for g in range(z.shape[1] // LANES):
        zg = z[:, g * LANES:(g + 1) * LANES]
        parts.append(zg * c + pltpu.roll(zg, LANES - ROT_HALF, 1) * s1
                     + pltpu.roll(zg, ROT_HALF, 1) * s2)
    return parts[0] if len(parts) == 1 else jnp.concatenate(parts, axis=1)


def _key_variants(k):
    lane = lax.broadcasted_iota(jnp.int32, k.shape, 1)
    lo = lane < HEAD_DIM
    kr = pltpu.roll(k, HEAD_DIM, 1)
    zero = jnp.zeros_like(k)
    return [jnp.where(lo, k, zero).astype(BF16), jnp.where(lo, zero, kr).astype(BF16),
            jnp.where(lo, kr, zero).astype(BF16), jnp.where(lo, zero, k).astype(BF16)]


def _meta_kernel(meta_ref, gpre_ref, wqT_ref, bq_ref, wk_ref, bk_ref, wvT_ref, bv_ref,
                 wglu_ref, bglu_ref, wgate_ref, bgate_ref, sink_ref, wap_ref, dww_ref,
                 dwb_ref, lng_ref, lnb_ref, wcp_ref, bcp_ref, wout_ref, gpost_ref,
                 gffn_ref, wup_ref, rc_ref, rs1_ref, rs2_ref,
                 k_out, v_out, glu_out, u_out, gscr):
    m = meta_ref[...]
    hn = _rms(m, gpre_ref[...]).astype(BF16)
    q = _dot_nt(hn, wqT_ref[...]) + bq_ref[...]
    k = _dot(hn, wk_ref[...]) + bk_ref[...]
    v = _dot_nt(hn, wvT_ref[...]) + bv_ref[...]
    rc, rs1, rs2 = rc_ref[...], rs1_ref[...], rs2_ref[...]
    q = _rope_rows(q, rc, rs1, rs2)
    k = _rope_rows(k, rc, rs1, rs2)
    k_out[...] = k
    v_out[...] = v

    row = lax.broadcasted_iota(jnp.int32, (N_META, N_META), 0)
    col = lax.broadcasted_iota(jnp.int32, (N_META, N_META), 1)
    causal = col <= row
    qb, kb, vb = q.astype(BF16), k.astype(BF16), v.astype(BF16)
    attn = jnp.zeros((N_META, D_MODEL), F32)
    for hq in range(N_Q_HEADS):
        h = hq // GROUP
        qh = qb[:, hq * HEAD_DIM:(hq + 1) * HEAD_DIM]
        kh = kb[:, h * HEAD_DIM:(h + 1) * HEAD_DIM]
        vh = vb[:, h * HEAD_DIM:(h + 1) * HEAD_DIM]
        s = jnp.where(causal, _dot_nt(qh, kh), NEG_INF)
        sink = sink_ref[hq]
        mx = jnp.maximum(jnp.max(s, axis=-1, keepdims=True), sink)
        p = jnp.exp(s - mx)
        den = jnp.sum(p, axis=-1, keepdims=True) + jnp.exp(sink - mx)
        o = _dot(p.astype(BF16), vh) / den
        attn = attn + _dot(o.astype(BF16), wap_ref[hq * HEAD_DIM:(hq + 1) * HEAD_DIM, :])

    glu_in = _dot(hn, wglu_ref[...]) + bglu_ref[...]
    glu = glu_in[:, :CONV_CH] * _sigmoid(glu_in[:, CONV_CH:])
    glu_out[...] = glu
    gscr[0:CONV_HALO, :] = jnp.zeros((CONV_HALO, CONV_CH), F32)
    gscr[CONV_HALO:CONV_HALO + N_META, :] = glu
    c = jnp.broadcast_to(dwb_ref[...], (N_META, CONV_CH))
    off = CONV_HALO - (CONV_K - 1)
    for kk in range(CONV_K):
        c = c + gscr[off + kk:off + kk + N_META, :] * dww_ref[kk:kk + 1, :]
    c = _silu(_layer_norm(c, lng_ref[...], lnb_ref[...]))
    conv = _dot(c.astype(BF16), wcp_ref[...]) + bcp_ref[...]

    gates = _sigmoid(_dot(hn, wgate_ref[...]) + bgate_ref[...])
    merged = gates[:, :D_MODEL] * attn + gates[:, D_MODEL:] * conv
    mix = _dot(merged.astype(BF16), wout_ref[...])
    h2 = m + _rms(mix, gpost_ref[...])
    hn2 = _rms(h2, gffn_ref[...]).astype(BF16)
    u_out[...] = _dot(hn2, wup_ref[...])


def _mixer_kernel(x_ref, gpre_ref, wqT_ref, bqT_ref, wk_ref, bk_ref, wvT_ref, bvT_ref,
                  wglu_ref, bglu_ref, wgate_ref, bgate_ref, sink_ref, wap_ref, dww_ref,
                  dwb_ref, lng_ref, lnb_ref, wcp_ref, bcp_ref, wout_ref, gpost_ref,
                  rc_ref, rs1_ref, rs2_ref, cosT_ref, sinT_ref,
                  kmeta_ref, vmetaT_ref, glumeta_ref,
                  out_ref,
                  kbuf, vTbuf, gbuf, shbuf, qT_s, oT_s, attn_s, cbuf, bias_s):
    T = MIX_TILE
    nblk = T // BLOCK
    t = pl.program_id(1)

    @pl.when(t == 0)
    def _init():
        kbuf[:, 0:BLOCK, :] = jnp.zeros((4, BLOCK, LANES), BF16)
        vTbuf[:, 0:BLOCK] = jnp.zeros((KV_WIDTH, BLOCK), BF16)
        gbuf[0:CONV_HALO - N_META, :] = jnp.zeros((CONV_HALO - N_META, CONV_CH), F32)
        gbuf[CONV_HALO - N_META:CONV_HALO, :] = glumeta_ref[...]
        key = lax.broadcasted_iota(jnp.int32, (2 * BLOCK, 2 * BLOCK), 0)
        qry = lax.broadcasted_iota(jnp.int32, (2 * BLOCK, 2 * BLOCK), 1) % BLOCK
        vis = (key > qry) & (key <= qry + WINDOW)
        bias_s[0] = jnp.where(vis, 0.0, NEG_INF).astype(F32)
        bias_s[1] = jnp.where(vis & (key >= BLOCK), 0.0, NEG_INF).astype(F32)

    x = x_ref[0]
    hn = _rms(x, gpre_ref[...]).astype(BF16)

    glu_in = _dot(hn, wglu_ref[...]) + bglu_ref[...]
    gbuf[CONV_HALO:CONV_HALO + T, :] = glu_in[:, :CONV_CH] * _sigmoid(glu_in[:, CONV_CH:])
    for s in range(1, CONV_SHIFTS):
        shbuf[s - 1] = gbuf[SUBLANES - s:SUBLANES - s + T + CONV_PAD, :]

    qT = _dot_nt(wqT_ref[...], hn) + bqT_ref[...]
    cosT, sinT = cosT_ref[...], sinT_ref[...]
    for hq in range(N_Q_HEADS):
        base = hq * HEAD_DIM
        r0 = qT[base:base + ROT_HALF]
        r1 = qT[base + ROT_HALF:base + ROT_DIM]
        rot = jnp.concatenate([r0 * cosT - r1 * sinT, r1 * cosT + r0 * sinT], axis=0)
        qT_s[base:base + ROT_DIM, :] = rot.astype(BF16)
        qT_s[base + ROT_DIM:base + HEAD_DIM, :] = qT[base + ROT_DIM:base + HEAD_DIM].astype(BF16)

    k = _dot(hn, wk_ref[...]) + bk_ref[...]
    k = _rope_rows(k, rc_ref[...], rs1_ref[...], rs2_ref[...])
    for i, kv in enumerate(_key_variants(k)):
        kbuf[i, BLOCK:BLOCK + T, :] = kv
    kmeta_var = _key_variants(kmeta_ref[...])
    vT = _dot_nt(wvT_ref[...], hn) + bvT_ref[...]
    vTbuf[:, BLOCK:BLOCK + T] = vT.astype(BF16)
    vmetaT = vmetaT_ref[...]

    lane2 = lax.broadcasted_iota(jnp.int32, (1, 2 * BLOCK), 1)
    first_sel = jnp.where(t == 0, 1, 0)

    items = [(j, gp, hl) for j in range(nblk) for gp in range(4) for hl in range(2)]

    def scores(item):
        j, gp, hl = item
        h = gp // 2
        c0 = j * BLOCK
        rhs = jnp.concatenate(
            [qT_s[256 * gp:256 * gp + 128, c0:c0 + BLOCK],
             qT_s[256 * gp + 128:256 * gp + 256, c0:c0 + BLOCK]], axis=1)
        s_loc = _dot(kbuf[2 * h + hl, c0:c0 + 2 * BLOCK, :], rhs)
        s_met = _dot(kmeta_var[2 * h + hl], rhs)
        return s_loc, s_met

    def finish(item, s_loc, s_met):
        j, gp, hl = item
        h = gp // 2
        c0 = j * BLOCK
        ha = 4 * gp + hl
        hb = ha + 2
        s_loc = s_loc + (bias_s[first_sel] if j == 0 else bias_s[0])
        sink = jnp.where(lane2 < BLOCK, sink_ref[ha], sink_ref[hb])
        mx = jnp.maximum(jnp.maximum(jnp.max(s_loc, axis=0, keepdims=True),
                                     jnp.max(s_met, axis=0, keepdims=True)), sink)
        p_loc = jnp.exp(s_loc - mx)
        p_met = jnp.exp(s_met - mx)
        den = (jnp.sum(p_loc, axis=0, keepdims=True)
               + jnp.sum(p_met, axis=0, keepdims=True) + jnp.exp(sink - mx))
        vT_h = vTbuf[h * HEAD_DIM:(h + 1) * HEAD_DIM, c0:c0 + 2 * BLOCK]
        vmT_h = vmetaT[h * HEAD_DIM:(h + 1) * HEAD_DIM, :]
        oT = _dot(vT_h, p_loc.astype(BF16)) + _dot(vmT_h, p_met.astype(BF16))
        oT = oT * (1.0 / den)
        oT_s[j, ha * HEAD_DIM:(ha + 1) * HEAD_DIM, :] = oT[:, :BLOCK]
        oT_s[j, hb * HEAD_DIM:(hb + 1) * HEAD_DIM, :] = oT[:, BLOCK:]

    def conv_block(cc, rc):
        cols = slice(cc * CONV_COLS, (cc + 1) * CONV_COLS)
        r0 = rc * CONV_ROWS
        nsub = CONV_ROWS // SUBLANES
        bias8 = jnp.broadcast_to(dwb_ref[:, cols], (SUBLANES, CONV_COLS))
        accs = [bias8] * nsub
        for d in range(CONV_K):
            a, s = divmod(d, SUBLANES)
            w = dww_ref[CONV_K - 1 - d, :, cols]
            for r in range(nsub):
                if s == 0:
                    lo = r0 + r * SUBLANES + CONV_HALO - SUBLANES * a
                    g = gbuf[lo:lo + SUBLANES, cols]
                else:
                    lo = r0 + r * SUBLANES + CONV_PAD - SUBLANES * a
                    g = shbuf[s - 1, lo:lo + SUBLANES, cols]
                accs[r] = accs[r] + g * w
        for r in range(nsub):
            cbuf[r0 + r * SUBLANES:r0 + (r + 1) * SUBLANES, cols] = accs[r]

    conv_blocks = [(cc, rc) for cc in range(CONV_CH // CONV_COLS) for rc in range(T // CONV_ROWS)]
    conv_per_item = -(-len(conv_blocks) // len(items))

    pending = [scores(it) for it in items[:ATTN_LOOKAHEAD]]
    gate_parts = []
    items_per_blk = len(items) // nblk
    for i, it in enumerate(items):
        if i + ATTN_LOOKAHEAD < len(items):
            pending.append(scores(items[i + ATTN_LOOKAHEAD]))
        finish(it, *pending.pop(0))
        for _ in range(conv_per_item):
            if conv_blocks:
                conv_block(*conv_blocks.pop(0))
        if (i + 1) % items_per_blk == 0:
            j = i // items_per_blk
            o_blk = oT_s[j].T.astype(BF16)
            attn_s[j * BLOCK:(j + 1) * BLOCK, :] = _dot(o_blk, wap_ref[...])
            half = slice(j * (2 * D_MODEL // nblk), (j + 1) * (2 * D_MODEL // nblk))
            gate_parts.append(_sigmoid(_dot(hn, wgate_ref[:, half]) + bgate_ref[:, half]))
    while conv_blocks:
        conv_block(*conv_blocks.pop(0))
    gates = jnp.concatenate(gate_parts, axis=1)

    kbuf[:, 0:BLOCK, :] = kbuf[:, T:T + BLOCK, :]
    vTbuf[:, 0:BLOCK] = vTbuf[:, T:T + BLOCK]
    gbuf[0:CONV_HALO, :] = gbuf[T:T + CONV_HALO, :]

    c = _silu(_layer_norm(cbuf[...], lng_ref[...], lnb_ref[...]))
    conv = _dot(c.astype(BF16), wcp_ref[...]) + bcp_ref[...]

    merged = gates[:, :D_MODEL] * attn_s[...] + gates[:, D_MODEL:] * conv
    mix = _dot(merged.astype(BF16), wout_ref[...])
    out_ref[0] = x + _rms(mix, gpost_ref[...])


def _ffn_cols(cidx, half):
    start = half * FFN_DIM + cidx * FFN_CHUNK
    return slice(start, start + FFN_CHUNK)


def _ffn_kernel(h_ref, gpre_ref, gpost_ref, wup_ref, dwp_ref, wdown_ref, umeta_ref,
                out_ref, hn_s, ubuf, carry, act_s, acc_s):
    T = FFN_TILE
    t = pl.program_id(1)

    @pl.when(t == 0)
    def _init():
        carry[...] = umeta_ref[...]

    h = h_ref[0]
    hn_s[...] = _rms(h, gpre_ref[...]).astype(BF16)

    def up(cidx):
        par = cidx % 2
        for half in range(2):
            cols = _ffn_cols(cidx, half)
            u = _dot(hn_s[...], wup_ref[:, cols])
            ubuf[par, half, 0:SUBLANES, :] = carry[:, cols]
            ubuf[par, half, SUBLANES:SUBLANES + T, :] = u
            carry[:, cols] = u[T - SUBLANES:T, :]

    def activate(cidx):
        par = cidx % 2
        ys = []
        for half in range(2):
            w = dwp_ref[:, _ffn_cols(cidx, half)]
            ys.append(w[3:4, :]
                      + w[0:1, :] * ubuf[par, half, SUBLANES - 2:SUBLANES - 2 + T, :]
                      + w[1:2, :] * ubuf[par, half, SUBLANES - 1:SUBLANES - 1 + T, :]
                      + w[2:3, :] * ubuf[par, half, SUBLANES:SUBLANES + T, :])
        act_s[:, cidx * FFN_CHUNK:(cidx + 1) * FFN_CHUNK] = (_silu(ys[0]) * ys[1]).astype(BF16)

    total = None
    up(0)
    for cidx in range(N_FFN_CHUNKS):
        if cidx + 1 < N_FFN_CHUNKS:
            up(cidx + 1)
        activate(cidx)
        for lo, hi in FFN_DOWN_GROUPS:
            if cidx == hi - 1:
                part = _dot(act_s[:, lo * FFN_CHUNK:hi * FFN_CHUNK],
                            wdown_ref[lo * FFN_CHUNK:hi * FFN_CHUNK, :])
                if hi == N_FFN_CHUNKS:
                    total = acc_s[...] + part
                elif lo == 0:
                    acc_s[...] = part
                else:
                    acc_s[...] += part
    out_ref[0] = h + _rms(total, gpost_ref[...])


def _const_spec(shape):
    nd = len(shape)
    return pl.BlockSpec(shape, lambda *_: (0,) * nd, pipeline_mode=pl.Buffered(1))


def _rope_tables(n_pos):
    inv_freq = ROPE_THETA ** (-jnp.arange(ROT_HALF, dtype=F32) * 2.0 / ROT_DIM)
    ang = jnp.arange(n_pos).astype(F32)[:, None] * inv_freq[None, :]
    cos, sin = jnp.cos(ang), jnp.sin(ang)
    ones = jnp.ones((n_pos, HEAD_DIM - ROT_DIM), F32)
    zeros = jnp.zeros((n_pos, HEAD_DIM - ROT_HALF), F32)
    c = jnp.concatenate([cos, cos, ones], axis=1)
    s1 = jnp.concatenate([-sin, zeros], axis=1)
    s2 = jnp.concatenate([zeros[:, :ROT_HALF], sin, zeros[:, :HEAD_DIM - ROT_DIM]], axis=1)
    rep = LANES // HEAD_DIM
    return (jnp.tile(c, (1, rep)), jnp.tile(s1, (1, rep)), jnp.tile(s2, (1, rep)),
            cos.T, sin.T)


def kernel(x, meta_tokens, norm_pre_mix, norm_post_mix, w_in, b_in, attn_sinks, w_attn_proj,
           conv_dw_w, conv_dw_b, conv_ln_g, conv_ln_b, w_conv_proj, b_conv_proj, w_out,
           norm_pre_ffn, norm_post_ffn, w_up, ffn_dw_w, ffn_dw_b, w_down):
    bsz, seq, _ = x.shape
    assert seq % MIX_TILE == 0 and seq % FFN_TILE == 0
    scale = HEAD_DIM ** -0.5
    row = lambda v: v.reshape(1, -1).astype(F32)

    wi, bi = w_in[0], b_in[0]
    c_q, c_k, c_v, c_glu = ATTN_WIDTH, ATTN_WIDTH + KV_WIDTH, ATTN_WIDTH + 2 * KV_WIDTH, \
        ATTN_WIDTH + 2 * KV_WIDTH + 2 * CONV_CH
    wqT = (wi[:, :c_q] * scale).T.astype(BF16)
    bq = bi[:c_q] * scale
    wk = wi[:, c_q:c_k].astype(BF16)
    bk = row(bi[c_q:c_k])
    wvT = wi[:, c_k:c_v].T.astype(BF16)
    bv = bi[c_k:c_v]
    wglu = wi[:, c_v:c_glu].astype(BF16)
    bglu = row(bi[c_v:c_glu])
    wgate = wi[:, c_glu:].astype(BF16)
    bgate = row(bi[c_glu:])
    sinks = attn_sinks[0].astype(F32)
    wap = w_attn_proj[0].astype(BF16)
    dww = conv_dw_w[0].astype(F32)
    dww8 = jnp.broadcast_to(dww[:, None, :], (CONV_K, SUBLANES, CONV_CH))
    dwb = row(conv_dw_b[0])
    lng, lnb = row(conv_ln_g[0]), row(conv_ln_b[0])
    wcp = w_conv_proj[0].astype(BF16)
    bcp = row(b_conv_proj[0])
    wout = w_out[0].astype(BF16)
    gpre, gpost = row(norm_pre_mix[0]), row(norm_post_mix[0])
    gffn, gffn_post = row(norm_pre_ffn[0]), row(norm_post_ffn[0])

    nch = 2 * N_FFN_CHUNKS
    wup = w_up[0].astype(BF16)
    dwp = jnp.concatenate(
        [ffn_dw_w[0], ffn_dw_b[0][None, :], jnp.zeros((SUBLANES - FFN_CONV_K - 1, 2 * FFN_DIM), F32)],
        axis=0)
    wdown = w_down[0].astype(BF16)

    rc, rs1, rs2, cosT, sinT = _rope_tables(N_META + seq)

    smem = pl.BlockSpec(memory_space=pltpu.SMEM)
    vmem = pl.BlockSpec(memory_space=pltpu.VMEM)

    k_meta, v_meta, glu_meta, u_meta = pl.pallas_call(
        _meta_kernel,
        out_shape=(jax.ShapeDtypeStruct((N_META, KV_WIDTH), F32),
                   jax.ShapeDtypeStruct((N_META, KV_WIDTH), F32),
                   jax.ShapeDtypeStruct((N_META, CONV_CH), F32),
                   jax.ShapeDtypeStruct((N_META, 2 * FFN_DIM), F32)),
        in_specs=[vmem] * 12 + [smem] + [vmem] * 14,
        out_specs=(vmem, vmem, vmem, vmem),
        scratch_shapes=[pltpu.VMEM((CONV_HALO + N_META, CONV_CH), F32)],
        compiler_params=pltpu.CompilerParams(vmem_limit_bytes=VMEM_LIMIT),
        name="meta_prologue",
    )(meta_tokens.astype(F32), gpre, wqT, row(bq), wk, bk, wvT, row(bv), wglu, bglu, wgate,
      bgate, sinks, wap, dww, dwb, lng, lnb, wcp, bcp, wout, gpost, gffn, wup,
      rc[:N_META], rs1[:N_META], rs2[:N_META])

    vmetaT = v_meta.T.astype(BF16)
    umeta_tail = u_meta[N_META - SUBLANES:, :]

    T = MIX_TILE
    bqT = jnp.broadcast_to(bq[:, None], (ATTN_WIDTH, T)).astype(F32)
    bvT = jnp.broadcast_to(bv[:, None], (KV_WIDTH, T)).astype(F32)
    tile_spec = pl.BlockSpec((1, T, D_MODEL), lambda b, t: (b, t, 0))
    rope_spec = pl.BlockSpec((T, LANES), lambda b, t: (t, 0))
    ropeT_spec = pl.BlockSpec((SUBLANES, T), lambda b, t: (0, t))
    mixer_in = [
        (x, tile_spec), (gpre, None), (wqT, None), (bqT, None), (wk, None), (bk, None),
        (wvT, None), (bvT, None), (wglu, None), (bglu, None), (wgate, None), (bgate, None),
        (sinks, smem), (wap, None), (dww8, None), (dwb, None), (lng, None), (lnb, None),
        (wcp, None), (bcp, None), (wout, None), (gpost, None),
        (rc[N_META:], rope_spec), (rs1[N_META:], rope_spec), (rs2[N_META:], rope_spec),
        (cosT[:, N_META:], ropeT_spec), (sinT[:, N_META:], ropeT_spec),
        (k_meta, None), (vmetaT, None), (glu_meta, None),
    ]
    h2 = pl.pallas_call(
        _mixer_kernel,
        out_shape=jax.ShapeDtypeStruct((bsz, seq, D_MODEL), F32),
        grid=(bsz, seq // T),
        in_specs=[spec if spec is not None else _const_spec(a.shape) for a, spec in mixer_in],
        out_specs=tile_spec,
        scratch_shapes=[
            pltpu.VMEM((4, BLOCK + T, LANES), BF16),
            pltpu.VMEM((KV_WIDTH, BLOCK + T), BF16),
            pltpu.VMEM((CONV_HALO + T, CONV_CH), F32),
            pltpu.VMEM((CONV_SHIFTS - 1, T + CONV_PAD, CONV_CH), F32),
            pltpu.VMEM((ATTN_WIDTH, T), BF16),
            pltpu.VMEM((T // BLOCK, ATTN_WIDTH, BLOCK), F32),
            pltpu.VMEM((T, D_MODEL), F32),
            pltpu.VMEM((T, CONV_CH), F32),
            pltpu.VMEM((2, 2 * BLOCK, 2 * BLOCK), F32),
        ],
        compiler_params=pltpu.CompilerParams(
            dimension_semantics=("arbitrary", "arbitrary"), vmem_limit_bytes=VMEM_LIMIT),
        name="mixer",
    )(*[a for a, _ in mixer_in])

    T2 = FFN_TILE
    tile2 = pl.BlockSpec((1, T2, D_MODEL), lambda b, t: (b, t, 0))
    ffn_in = [(h2, tile2), (gffn, None), (gffn_post, None), (wup, None), (dwp, None),
              (wdown, None), (umeta_tail, None)]
    out = pl.pallas_call(
        _ffn_kernel,
        out_shape=jax.ShapeDtypeStruct((bsz, seq, D_MODEL), F32),
        grid=(bsz, seq // T2),
        in_specs=[spec if spec is not None else _const_spec(a.shape) for a, spec in ffn_in],
        out_specs=tile2,
        scratch_shapes=[
            pltpu.VMEM((T2, D_MODEL), BF16),
            pltpu.VMEM((2, 2, SUBLANES + T2, FFN_CHUNK), F32),
            pltpu.VMEM((SUBLANES, 2 * FFN_DIM), F32),
            pltpu.VMEM((T2, FFN_DIM), BF16),
            pltpu.VMEM((T2, D_MODEL), F32),
        ],
        compiler_params=pltpu.CompilerParams(
            dimension_semantics=("arbitrary", "arbitrary"), vmem_limit_bytes=VMEM_LIMIT),
        name="conv_ffn",
    )(*[a for a, _ in ffn_in])
    return out.astype(x.dtype)
```

```python
import functools

import numpy as np
import jax
import jax.numpy as jnp
from jax import lax
from jax.experimental import pallas as pl
from jax.experimental.pallas import tpu as pltpu

D_MODEL = 1024
N_META = 16
N_Q_HEADS = 16
N_KV_HEADS = 2
HEAD_DIM = 64
GROUP = N_Q_HEADS // N_KV_HEADS
ROT_DIM = HEAD_DIM // 4
ROT_HALF = ROT_DIM // 2
ROPE_THETA = 500000.0
WINDOW = 128
BLOCK = 128
ATTN_WIDTH = N_Q_HEADS * HEAD_DIM
KV_WIDTH = N_KV_HEADS * HEAD_DIM
CONV_CH = D_MODEL
CONV_K = 31
FFN_DIM = 2816
FFN_CONV_K = 3
RMS_EPS = 1e-6
LN_EPS = 1e-5
NEG_INF = -1e30

LANES = 128
SUBLANES = 8
MIX_TILE = 512
FFN_TILE = 512
FFN_CHUNK = 256
N_FFN_CHUNKS = FFN_DIM // FFN_CHUNK
FFN_DOWN_GROUPS = ((0, 3), (3, 6), (6, 9), (9, 11))
FFN_UP_AHEAD = 2
CONV_HALO = 32
CONV_SHIFTS = SUBLANES
CONV_PAD = CONV_HALO - SUBLANES
CONV_ROWS = 32
CONV_COLS = 256
ATTN_LOOKAHEAD = 2
GATE_CHUNKS = 4
VMEM_LIMIT = 56 * 1024 * 1024

F32 = jnp.float32
BF16 = jnp.bfloat16

_NT = (((1,), (1,)), ((), ()))


def _dot(a, b):
    return jnp.dot(a, b, preferred_element_type=F32)


def _dot_nt(a, b):
    return lax.dot_general(a, b, _NT, preferred_element_type=F32)


def _rms(x, g):
    ms = jnp.mean(x * x, axis=-1, keepdims=True)
    return x * lax.rsqrt(ms + RMS_EPS) * g


def _layer_norm(x, g, b):
    mu = jnp.mean(x, axis=-1, keepdims=True)
    xc = x - mu
    var = jnp.mean(xc * xc, axis=-1, keepdims=True)
    return xc * lax.rsqrt(var + LN_EPS) * g + b


def _sigmoid(x):
    return 0.5 * jnp.tanh(0.5 * x) + 0.5


def _silu(x):
    hx = 0.5 * x
    return hx * jnp.tanh(hx) + hx


def _rope_rows(z, c, s1, s2):
    parts = []
    for g in range(z.shape[1] // LANES):
        zg = z[:, g * LANES:(g + 1) * LANES]
        parts.append(zg * c + pltpu.roll(zg, LANES - ROT_HALF, 1) * s1
                     + pltpu.roll(zg, ROT_HALF, 1) * s2)
    return parts[0] if len(parts) == 1 else jnp.concatenate(parts, axis=1)


def _key_variants(k):
    lane = lax.broadcasted_iota(jnp.int32, k.shape, 1)
    lo = lane < HEAD_DIM
    kr = pltpu.roll(k, HEAD_DIM, 1)
    zero = jnp.zeros_like(k)
    return [jnp.where(lo, k, zero).astype(BF16), jnp.where(lo, zero, kr).astype(BF16),
            jnp.where(lo, kr, zero).astype(BF16), jnp.where(lo, zero, k).astype(BF16)]


def _meta_kernel(meta_ref, gpre_ref, wqT_ref, bq_ref, wk_ref, bk_ref, wvT_ref, bv_ref,
                 wglu_ref, bglu_ref, wgate_ref, bgate_ref, sink_ref, wap_ref, dww_ref,
                 dwb_ref, lng_ref, lnb_ref, wcp_ref, bcp_ref, wout_ref, gpost_ref,
                 gffn_ref, wup_ref, rc_ref, rs1_ref, rs2_ref,
                 k_out, v_out, glu_out, u_out, gscr):
    m = meta_ref[...]
    hn = _rms(m, gpre_ref[...]).astype(BF16)
    q = _dot_nt(hn, wqT_ref[...]) + bq_ref[...]
    k = _dot(hn, wk_ref[...]) + bk_ref[...]
    v = _dot_nt(hn, wvT_ref[...]) + bv_ref[...]
    rc, rs1, rs2 = rc_ref[...], rs1_ref[...], rs2_ref[...]
    q = _rope_rows(q, rc, rs1, rs2)
    k = _rope_rows(k, rc, rs1, rs2)
    k_out[...] = k
    v_out[...] = v

    row = lax.broadcasted_iota(jnp.int32, (N_META, N_META), 0)
    col = lax.broadcasted_iota(jnp.int32, (N_META, N_META), 1)
    causal = col <= row
    qb, kb, vb = q.astype(BF16), k.astype(BF16), v.astype(BF16)
    attn = jnp.zeros((N_META, D_MODEL), F32)
    for hq in range(N_Q_HEADS):
        h = hq // GROUP
        qh = qb[:, hq * HEAD_DIM:(hq + 1) * HEAD_DIM]
        kh = kb[:, h * HEAD_DIM:(h + 1) * HEAD_DIM]
        vh = vb[:, h * HEAD_DIM:(h + 1) * HEAD_DIM]
        s = jnp.where(causal, _dot_nt(qh, kh), NEG_INF)
        sink = sink_ref[hq]
        mx = jnp.maximum(jnp.max(s, axis=-1, keepdims=True), sink)
        p = jnp.exp(s - mx)
        den = jnp.sum(p, axis=-1, keepdims=True) + jnp.exp(sink - mx)
        o = _dot(p.astype(BF16), vh) / den
        attn = attn + _dot(o.astype(BF16), wap_ref[hq * HEAD_DIM:(hq + 1) * HEAD_DIM, :])

    glu_in = _dot(hn, wglu_ref[...]) + bglu_ref[...]
    glu = glu_in[:, :CONV_CH] * _sigmoid(glu_in[:, CONV_CH:])
    glu_out[...] = glu
    gscr[0:CONV_HALO, :] = jnp.zeros((CONV_HALO, CONV_CH), F32)
    gscr[CONV_HALO:CONV_HALO + N_META, :] = glu
    c = jnp.broadcast_to(dwb_ref[...], (N_META, CONV_CH))
    off = CONV_HALO - (CONV_K - 1)
    for kk in range(CONV_K):
        c = c + gscr[off + kk:off + kk + N_META, :] * dww_ref[kk:kk + 1, :]
    c = _silu(_layer_norm(c, lng_ref[...], lnb_ref[...]))
    conv = _dot(c.astype(BF16), wcp_ref[...]) + bcp_ref[...]

    gates = _sigmoid(_dot(hn, wgate_ref[...]) + bgate_ref[...])
    merged = gates[:, :D_MODEL] * attn + gates[:, D_MODEL:] * conv
    mix = _dot(merged.astype(BF16), wout_ref[...])
    h2 = m + _rms(mix, gpost_ref[...])
    hn2 = _rms(h2, gffn_ref[...]).astype(BF16)
    u_out[...] = _dot(hn2, wup_ref[...])


def _mixer_kernel(x_ref, gpre_ref, wqT_ref, bqT_ref, wk_ref, bk_ref, wvT_ref, bvT_ref,
                  wglu_ref, bglu_ref, wgate_ref, bgate_ref, sink_ref, wap_ref, dww_ref,
                  dwb_ref, lng_ref, lnb_ref, wcp_ref, bcp_ref, wout_ref, gpost_ref,
                  rc_ref, rs1_ref, rs2_ref, cosT_ref, sinT_ref,
                  kmeta_ref, vmetaT_ref, glumeta_ref,
                  out_ref,
                  kbuf, vTbuf, gbuf, shbuf, qT_s, oT_s, attn_s, cbuf, bias_s):
    T = MIX_TILE
    nblk = T // BLOCK
    t = pl.program_id(1)

    @pl.when(t == 0)
    def _init():
        kbuf[:, 0:BLOCK, :] = jnp.zeros((4, BLOCK, LANES), BF16)
        vTbuf[:, 0:BLOCK] = jnp.zeros((KV_WIDTH, BLOCK), BF16)
        gbuf[0:CONV_HALO - N_META, :] = jnp.zeros((CONV_HALO - N_META, CONV_CH), F32)
        gbuf[CONV_HALO - N_META:CONV_HALO, :] = glumeta_ref[...]
        key = lax.broadcasted_iota(jnp.int32, (2 * BLOCK, 2 * BLOCK), 0)
        qry = lax.broadcasted_iota(jnp.int32, (2 * BLOCK, 2 * BLOCK), 1) % BLOCK
        vis = (key > qry) & (key <= qry + WINDOW)
        bias_s[0] = jnp.where(vis, 0.0, NEG_INF).astype(F32)
        bias_s[1] = jnp.where(vis & (key >= BLOCK), 0.0, NEG_INF).astype(F32)

    x = x_ref[0]
    hn = _rms(x, gpre_ref[...]).astype(BF16)

    glu_in = _dot(hn, wglu_ref[...]) + bglu_ref[...]
    gbuf[CONV_HALO:CONV_HALO + T, :] = glu_in[:, :CONV_CH] * _sigmoid(glu_in[:, CONV_CH:])
    for s in range(1, CONV_SHIFTS):
        shbuf[s - 1] = gbuf[SUBLANES - s:SUBLANES - s + T + CONV_PAD, :]

    qT = _dot_nt(wqT_ref[...], hn) + bqT_ref[...]
    cosT, sinT = cosT_ref[...], sinT_ref[...]
    for hq in range(N_Q_HEADS):
        base = hq * HEAD_DIM
        r0 = qT[base:base + ROT_HALF]
        r1 = qT[base + ROT_HALF:base + ROT_DIM]
        rot = jnp.concatenate([r0 * cosT - r1 * sinT, r1 * cosT + r0 * sinT], axis=0)
        qT_s[base:base + ROT_DIM, :] = rot.astype(BF16)
        qT_s[base + ROT_DIM:base + HEAD_DIM, :] = qT[base + ROT_DIM:base + HEAD_DIM].astype(BF16)

    k = _dot(hn, wk_ref[...]) + bk_ref[...]
    k = _rope_rows(k, rc_ref[...], rs1_ref[...], rs2_ref[...])
    for i, kv in enumerate(_key_variants(k)):
        kbuf[i, BLOCK:BLOCK + T, :] = kv
    kmeta_var = _key_variants(kmeta_ref[...])
    vT = _dot_nt(wvT_ref[...], hn) + bvT_ref[...]
    vTbuf[:, BLOCK:BLOCK + T] = vT.astype(BF16)
    vmetaT = vmetaT_ref[...]

    lane2 = lax.broadcasted_iota(jnp.int32, (1, 2 * BLOCK), 1)
    first_sel = jnp.where(t == 0, 1, 0)

    items = [(j, gp, hl) for j in range(nblk) for gp in range(4) for hl in range(2)]

    def scores(item):
        j, gp, hl = item
        h = gp // 2
        c0 = j * BLOCK
        rhs = jnp.concatenate(
            [qT_s[256 * gp:256 * gp + 128, c0:c0 + BLOCK],
             qT_s[256 * gp + 128:256 * gp + 256, c0:c0 + BLOCK]], axis=1)
        s_loc = _dot(kbuf[2 * h + hl, c0:c0 + 2 * BLOCK, :], rhs)
        s_met = _dot(kmeta_var[2 * h + hl], rhs)
        return s_loc, s_met

    def finish(item, s_loc, s_met):
        j, gp, hl = item
        h = gp // 2
        c0 = j * BLOCK
        ha = 4 * gp + hl
        hb = ha + 2
        s_loc = s_loc + (bias_s[first_sel] if j == 0 else bias_s[0])
        sink = jnp.where(lane2 < BLOCK, sink_ref[ha], sink_ref[hb])
        mx = jnp.maximum(jnp.maximum(jnp.max(s_loc, axis=0, keepdims=True),
                                     jnp.max(s_met, axis=0, keepdims=True)), sink)
        p_loc = jnp.exp(s_loc - mx)
        p_met = jnp.exp(s_met - mx)
        den = (jnp.sum(p_loc, axis=0, keepdims=True)
               + jnp.sum(p_met, axis=0, keepdims=True) + jnp.exp(sink - mx))
        vT_h = vTbuf[h * HEAD_DIM:(h + 1) * HEAD_DIM, c0:c0 + 2 * BLOCK]
        vmT_h = vmetaT[h * HEAD_DIM:(h + 1) * HEAD_DIM, :]
        oT = _dot(vT_h, p_loc.astype(BF16)) + _dot(vmT_h, p_met.astype(BF16))
        oT = oT * (1.0 / den)
        oT_s[j, ha * HEAD_DIM:(ha + 1) * HEAD_DIM, :] = oT[:, :BLOCK]
        oT_s[j, hb * HEAD_DIM:(hb + 1) * HEAD_DIM, :] = oT[:, BLOCK:]

    def conv_block(cc, rc):
        cols = slice(cc * CONV_COLS, (cc + 1) * CONV_COLS)
        r0 = rc * CONV_ROWS
        nsub = CONV_ROWS // SUBLANES
        bias8 = jnp.broadcast_to(dwb_ref[:, cols], (SUBLANES, CONV_COLS))
        accs = [bias8] * nsub
        for d in range(CONV_K):
            a, s = divmod(d, SUBLANES)
            w = dww_ref[CONV_K - 1 - d, :, cols]
            for r in range(nsub):
                if s == 0:
                    lo = r0 + r * SUBLANES + CONV_HALO - SUBLANES * a
                    g = gbuf[lo:lo + SUBLANES, cols]
                else:
                    lo = r0 + r * SUBLANES + CONV_PAD - SUBLANES * a
                    g = shbuf[s - 1, lo:lo + SUBLANES, cols]
                accs[r] = accs[r] + g * w
        for r in range(nsub):
            cbuf[r0 + r * SUBLANES:r0 + (r + 1) * SUBLANES, cols] = accs[r]

    conv_blocks = [(cc, rc) for cc in range(CONV_CH // CONV_COLS) for rc in range(T // CONV_ROWS)]
    conv_per_item = -(-len(conv_blocks) // len(items))

    pending = [scores(it) for it in items[:ATTN_LOOKAHEAD]]
    gate_parts = []
    items_per_blk = len(items) // nblk
    items_per_gate = len(items) // GATE_CHUNKS
    gate_w = 2 * D_MODEL // GATE_CHUNKS
    for i, it in enumerate(items):
        if i + ATTN_LOOKAHEAD < len(items):
            pending.append(scores(items[i + ATTN_LOOKAHEAD]))
        finish(it, *pending.pop(0))
        for _ in range(conv_per_item):
            if conv_blocks:
                conv_block(*conv_blocks.pop(0))
        if (i + 1) % items_per_gate == 0:
            gcols = slice(i // items_per_gate * gate_w, (i // items_per_gate + 1) * gate_w)
            gate_parts.append(_sigmoid(_dot(hn, wgate_ref[:, gcols]) + bgate_ref[:, gcols]))
        if (i + 1) % items_per_blk == 0:
            j = i // items_per_blk
            o_blk = oT_s[j].T.astype(BF16)
            attn_s[j * BLOCK:(j + 1) * BLOCK, :] = _dot(o_blk, wap_ref[...])
    while conv_blocks:
        conv_block(*conv_blocks.pop(0))
    gates = jnp.concatenate(gate_parts, axis=1)

    kbuf[:, 0:BLOCK, :] = kbuf[:, T:T + BLOCK, :]
    vTbuf[:, 0:BLOCK] = vTbuf[:, T:T + BLOCK]
    gbuf[0:CONV_HALO, :] = gbuf[T:T + CONV_HALO, :]

    c = _silu(_layer_norm(cbuf[...], lng_ref[...], lnb_ref[...]))
    conv = _dot(c.astype(BF16), wcp_ref[...]) + bcp_ref[...]

    merged = gates[:, :D_MODEL] * attn_s[...] + gates[:, D_MODEL:] * conv
    mix = _dot(merged.astype(BF16), wout_ref[...])
    out_ref[0] = x + _rms(mix, gpost_ref[...])


def _ffn_cols(cidx, half):
    start = half * FFN_DIM + cidx * FFN_CHUNK
    return slice(start, start + FFN_CHUNK)


def _ffn_kernel(h_ref, gpre_ref, gpost_ref, wup_ref, dwp_ref, wdown_ref, umeta_ref,
                out_ref, hn_s, ubuf, carry, act_s, acc_s):
    T = FFN_TILE
    t = pl.program_id(1)

    @pl.when(t == 0)
    def _init():
        carry[...] = umeta_ref[...]

    h = h_ref[0]
    hn_s[...] = _rms(h, gpre_ref[...]).astype(BF16)

    def up(cidx):
        par = cidx % (FFN_UP_AHEAD + 1)
        for half in range(2):
            cols = _ffn_cols(cidx, half)
            u = _dot(hn_s[...], wup_ref[:, cols])
            ubuf[par, half, 0:SUBLANES, :] = carry[:, cols]
            ubuf[par, half, SUBLANES:SUBLANES + T, :] = u
            carry[:, cols] = u[T - SUBLANES:T, :]

    def activate(cidx):
        par = cidx % (FFN_UP_AHEAD + 1)
        ys = []
        for half in range(2):
            w = dwp_ref[:, _ffn_cols(cidx, half)]
            ue = ubuf[par, half]
            delayed = [pltpu.roll(ue, d, 0)[SUBLANES:SUBLANES + T, :] if d else ue[SUBLANES:SUBLANES + T, :]
                       for d in range(FFN_CONV_K)]
            ys.append(w[3:4, :] + w[0:1, :] * delayed[2] + w[1:2, :] * delayed[1]
                      + w[2:3, :] * delayed[0])
        act_s[:, cidx * FFN_CHUNK:(cidx + 1) * FFN_CHUNK] = (_silu(ys[0]) * ys[1]).astype(BF16)

    total = None
    for cidx in range(FFN_UP_AHEAD):
        up(cidx)
    for cidx in range(N_FFN_CHUNKS):
        if cidx + FFN_UP_AHEAD < N_FFN_CHUNKS:
            up(cidx + FFN_UP_AHEAD)
        activate(cidx)
        for lo, hi in FFN_DOWN_GROUPS:
            if cidx == hi - 1:
                part = _dot(act_s[:, lo * FFN_CHUNK:hi * FFN_CHUNK],
                            wdown_ref[lo * FFN_CHUNK:hi * FFN_CHUNK, :])
                if hi == N_FFN_CHUNKS:
                    total = acc_s[...] + part
                elif lo == 0:
                    acc_s[...] = part
                else:
                    acc_s[...] += part
    out_ref[0] = h + _rms(total, gpost_ref[...])


def _const_spec(shape):
    nd = len(shape)
    return pl.BlockSpec(shape, lambda *_: (0,) * nd, pipeline_mode=pl.Buffered(1))


def _rope_tables(n_pos):
    inv_freq = ROPE_THETA ** (-jnp.arange(ROT_HALF, dtype=F32) * 2.0 / ROT_DIM)
    ang = jnp.arange(n_pos).astype(F32)[:, None] * inv_freq[None, :]
    cos, sin = jnp.cos(ang), jnp.sin(ang)
    ones = jnp.ones((n_pos, HEAD_DIM - ROT_DIM), F32)
    zeros = jnp.zeros((n_pos, HEAD_DIM - ROT_HALF), F32)
    c = jnp.concatenate([cos, cos, ones], axis=1)
    s1 = jnp.concatenate([-sin, zeros], axis=1)
    s2 = jnp.concatenate([zeros[:, :ROT_HALF], sin, zeros[:, :HEAD_DIM - ROT_DIM]], axis=1)
    rep = LANES // HEAD_DIM
    return (jnp.tile(c, (1, rep)), jnp.tile(s1, (1, rep)), jnp.tile(s2, (1, rep)),
            cos.T, sin.T)


def kernel(x, meta_tokens, norm_pre_mix, norm_post_mix, w_in, b_in, attn_sinks, w_attn_proj,
           conv_dw_w, conv_dw_b, conv_ln_g, conv_ln_b, w_conv_proj, b_conv_proj, w_out,
           norm_pre_ffn, norm_post_ffn, w_up, ffn_dw_w, ffn_dw_b, w_down):
    bsz, seq, _ = x.shape
    assert seq % MIX_TILE == 0 and seq % FFN_TILE == 0
    scale = HEAD_DIM ** -0.5
    row = lambda v: v.reshape(1, -1).astype(F32)

    wi, bi = w_in[0], b_in[0]
    c_q, c_k, c_v, c_glu = ATTN_WIDTH, ATTN_WIDTH + KV_WIDTH, ATTN_WIDTH + 2 * KV_WIDTH, \
        ATTN_WIDTH + 2 * KV_WIDTH + 2 * CONV_CH
    wqT = (wi[:, :c_q] * scale).T.astype(BF16)
    bq = bi[:c_q] * scale
    wk = wi[:, c_q:c_k].astype(BF16)
    bk = row(bi[c_q:c_k])
    wvT = wi[:, c_k:c_v].T.astype(BF16)
    bv = bi[c_k:c_v]
    wglu = wi[:, c_v:c_glu].astype(BF16)
    bglu = row(bi[c_v:c_glu])
    wgate = wi[:, c_glu:].astype(BF16)
    bgate = row(bi[c_glu:])
    sinks = attn_sinks[0].astype(F32)
    wap = w_attn_proj[0].astype(BF16)
    dww = conv_dw_w[0].astype(F32)
    dww8 = jnp.broadcast_to(dww[:, None, :], (CONV_K, SUBLANES, CONV_CH))
    dwb = row(conv_dw_b[0])
    lng, lnb = row(conv_ln_g[0]), row(conv_ln_b[0])
    wcp = w_conv_proj[0].astype(BF16)
    bcp = row(b_conv_proj[0])
    wout = w_out[0].astype(BF16)
    gpre, gpost = row(norm_pre_mix[0]), row(norm_post_mix[0])
    gffn, gffn_post = row(norm_pre_ffn[0]), row(norm_post_ffn[0])

    nch = 2 * N_FFN_CHUNKS
    wup = w_up[0].astype(BF16)
    dwp = jnp.concatenate(
        [ffn_dw_w[0], ffn_dw_b[0][None, :], jnp.zeros((SUBLANES - FFN_CONV_K - 1, 2 * FFN_DIM), F32)],
        axis=0)
    wdown = w_down[0].astype(BF16)

    rc, rs1, rs2, cosT, sinT = _rope_tables(N_META + seq)

    smem = pl.BlockSpec(memory_space=pltpu.SMEM)
    vmem = pl.BlockSpec(memory_space=pltpu.VMEM)

    k_meta, v_meta, glu_meta, u_meta = pl.pallas_call(
        _meta_kernel,
        out_shape=(jax.ShapeDtypeStruct((N_META, KV_WIDTH), F32),
                   jax.ShapeDtypeStruct((N_META, KV_WIDTH), F32),
                   jax.ShapeDtypeStruct((N_META, CONV_CH), F32),
                   jax.ShapeDtypeStruct((N_META, 2 * FFN_DIM), F32)),
        in_specs=[vmem] * 12 + [smem] + [vmem] * 14,
        out_specs=(vmem, vmem, vmem, vmem),
        scratch_shapes=[pltpu.VMEM((CONV_HALO + N_META, CONV_CH), F32)],
        compiler_params=pltpu.CompilerParams(vmem_limit_bytes=VMEM_LIMIT),
        name="meta_prologue",
    )(meta_tokens.astype(F32), gpre, wqT, row(bq), wk, bk, wvT, row(bv), wglu, bglu, wgate,
      bgate, sinks, wap, dww, dwb, lng, lnb, wcp, bcp, wout, gpost, gffn, wup,
      rc[:N_META], rs1[:N_META], rs2[:N_META])

    vmetaT = v_meta.T.astype(BF16)
    umeta_tail = u_meta[N_META - SUBLANES:, :]

    T = MIX_TILE
    bqT = jnp.broadcast_to(bq[:, None], (ATTN_WIDTH, T)).astype(F32)
    bvT = jnp.broadcast_to(bv[:, None], (KV_WIDTH, T)).astype(F32)
    tile_spec = pl.BlockSpec((1, T, D_MODEL), lambda b, t: (b, t, 0))
    rope_spec = pl.BlockSpec((T, LANES), lambda b, t: (t, 0))
    ropeT_spec = pl.BlockSpec((SUBLANES, T), lambda b, t: (0, t))
    mixer_in = [
        (x, tile_spec), (gpre, None), (wqT, None), (bqT, None), (wk, None), (bk, None),
        (wvT, None), (bvT, None), (wglu, None), (bglu, None), (wgate, None), (bgate, None),
        (sinks, smem), (wap, None), (dww8, None), (dwb, None), (lng, None), (lnb, None),
        (wcp, None), (bcp, None), (wout, None), (gpost, None),
        (rc[N_META:], rope_spec), (rs1[N_META:], rope_spec), (rs2[N_META:], rope_spec),
        (cosT[:, N_META:], ropeT_spec), (sinT[:, N_META:], ropeT_spec),
        (k_meta, None), (vmetaT, None), (glu_meta, None),
    ]
    h2 = pl.pallas_call(
        _mixer_kernel,
        out_shape=jax.ShapeDtypeStruct((bsz, seq, D_MODEL), F32),
        grid=(bsz, seq // T),
        in_specs=[spec if spec is not None else _const_spec(a.shape) for a, spec in mixer_in],
        out_specs=tile_spec,
        scratch_shapes=[
            pltpu.VMEM((4, BLOCK + T, LANES), BF16),
            pltpu.VMEM((KV_WIDTH, BLOCK + T), BF16),
            pltpu.VMEM((CONV_HALO + T, CONV_CH), F32),
            pltpu.VMEM((CONV_SHIFTS - 1, T + CONV_PAD, CONV_CH), F32),
            pltpu.VMEM((ATTN_WIDTH, T), BF16),
            pltpu.VMEM((T // BLOCK, ATTN_WIDTH, BLOCK), F32),
            pltpu.VMEM((T, D_MODEL), F32),
            pltpu.VMEM((T, CONV_CH), F32),
            pltpu.VMEM((2, 2 * BLOCK, 2 * BLOCK), F32),
        ],
        compiler_params=pltpu.CompilerParams(
            dimension_semantics=("arbitrary", "arbitrary"), vmem_limit_bytes=VMEM_LIMIT),
        name="mixer",
    )(*[a for a, _ in mixer_in])

    T2 = FFN_TILE
    tile2 = pl.BlockSpec((1, T2, D_MODEL), lambda b, t: (b, t, 0))
    ffn_in = [(h2, tile2), (gffn, None), (gffn_post, None), (wup, None), (dwp, None),
              (wdown, None), (umeta_tail, None)]
    out = pl.pallas_call(
        _ffn_kernel,
        out_shape=jax.ShapeDtypeStruct((bsz, seq, D_MODEL), F32),
        grid=(bsz, seq // T2),
        in_specs=[spec if spec is not None else _const_spec(a.shape) for a, spec in ffn_in],
        out_specs=tile2,
        scratch_shapes=[
            pltpu.VMEM((T2, D_MODEL), BF16),
            pltpu.VMEM((FFN_UP_AHEAD + 1, 2, SUBLANES + T2, FFN_CHUNK), F32),
            pltpu.VMEM((SUBLANES, 2 * FFN_DIM), F32),
            pltpu.VMEM((T2, FFN_DIM), BF16),
            pltpu.VMEM((T2, D_MODEL), F32),
        ],
        compiler_params=pltpu.CompilerParams(
            dimension_semantics=("arbitrary", "arbitrary"), vmem_limit_bytes=VMEM_LIMIT),
        name="conv_ffn",
    )(*[a for a, _ in ffn_in])
    return out.astype(x.dtype)
```

```python
import functools

import numpy as np
import jax
import jax.numpy as jnp
from jax import lax
from jax.experimental import pallas as pl
from jax.experimental.pallas import tpu as pltpu

D_MODEL = 1024
N_META = 16
N_Q_HEADS = 16
N_KV_HEADS = 2
HEAD_DIM = 64
GROUP = N_Q_HEADS // N_KV_HEADS
ROT_DIM = HEAD_DIM // 4
ROT_HALF = ROT_DIM // 2
ROPE_THETA = 500000.0
WINDOW = 128
BLOCK = 128
ATTN_WIDTH = N_Q_HEADS * HEAD_DIM
KV_WIDTH = N_KV_HEADS * HEAD_DIM
CONV_CH = D_MODEL
CONV_K = 31
FFN_DIM = 2816
FFN_CONV_K = 3
RMS_EPS = 1e-6
LN_EPS = 1e-5
NEG_INF = -1e30

LANES = 128
SUBLANES = 8
MIX_TILE = 512
FFN_TILE = 512
FFN_CHUNK = 256
N_FFN_CHUNKS = FFN_DIM // FFN_CHUNK
FFN_DOWN_GROUPS = ((0, 3), (3, 6), (6, 9), (9, 11))
FFN_UP_AHEAD = 2
CONV_HALO = 32
CONV_SHIFTS = SUBLANES
CONV_PAD = CONV_HALO - SUBLANES
CONV_ROWS = 32
CONV_COLS = 256
ATTN_LOOKAHEAD = 2
GATE_CHUNKS = 4
PV_DELAY = 2
ATTN_PROJ_DELAY = 0
VMEM_LIMIT = 56 * 1024 * 1024

F32 = jnp.float32
BF16 = jnp.bfloat16

_NT = (((1,), (1,)), ((), ()))


def _dot(a, b):
    return jnp.dot(a, b, preferred_element_type=F32)


def _dot_nt(a, b):
    return lax.dot_general(a, b, _NT, preferred_element_type=F32)


def _rms(x, g):
    ms = jnp.mean(x * x, axis=-1, keepdims=True)
    return x * lax.rsqrt(ms + RMS_EPS) * g


def _layer_norm(x, g, b):
    mu = jnp.mean(x, axis=-1, keepdims=True)
    xc = x - mu
    var = jnp.mean(xc * xc, axis=-1, keepdims=True)
    return xc * lax.rsqrt(var + LN_EPS) * g + b


def _sigmoid_2x(hx):
    return 0.5 * jnp.tanh(hx) + 0.5


def _silu_2x(hx):
    return hx * jnp.tanh(hx) + hx


def _rope_rows(z, c, s1, s2):
    parts = []
    for g in range(z.shape[1] // LANES):
        zg = z[:, g * LANES:(g + 1) * LANES]
        parts.append(zg * c + pltpu.roll(zg, LANES - ROT_HALF, 1) * s1
                     + pltpu.roll(zg, ROT_HALF, 1) * s2)
    return parts[0] if len(parts) == 1 else jnp.concatenate(parts, axis=1)


def _key_variants(k):
    lane = lax.broadcasted_iota(jnp.int32, k.shape, 1)
    lo = lane < HEAD_DIM
    kr = pltpu.roll(k, HEAD_DIM, 1)
    zero = jnp.zeros_like(k)
    return [jnp.where(lo, k, zero).astype(BF16), jnp.where(lo, zero, kr).astype(BF16),
            jnp.where(lo, kr, zero).astype(BF16), jnp.where(lo, zero, k).astype(BF16)]


def _meta_kernel(meta_ref, gpre_ref, wqT_ref, bq_ref, wk_ref, bk_ref, wvT_ref, bv_ref,
                 wglu_ref, bglu_ref, wgate_ref, bgate_ref, sink_ref, wap_ref, dww_ref,
                 dwb_ref, lng_ref, lnb_ref, wcp_ref, bcp_ref, wout_ref, gpost_ref,
                 gffn_ref, wup_ref, rc_ref, rs1_ref, rs2_ref,
                 k_out, v_out, glu_out, u_out, gscr):
    m = meta_ref[...]
    hn = _rms(m, gpre_ref[...]).astype(BF16)
    q = _dot_nt(hn, wqT_ref[...]) + bq_ref[...]
    k = _dot(hn, wk_ref[...]) + bk_ref[...]
    v = _dot_nt(hn, wvT_ref[...]) + bv_ref[...]
    rc, rs1, rs2 = rc_ref[...], rs1_ref[...], rs2_ref[...]
    q = _rope_rows(q, rc, rs1, rs2)
    k = _rope_rows(k, rc, rs1, rs2)
    k_out[...] = k
    v_out[...] = v

    row = lax.broadcasted_iota(jnp.int32, (N_META, N_META), 0)
    col = lax.broadcasted_iota(jnp.int32, (N_META, N_META), 1)
    causal = col <= row
    qb, kb, vb = q.astype(BF16), k.astype(BF16), v.astype(BF16)
    attn = jnp.zeros((N_META, D_MODEL), F32)
    for hq in range(N_Q_HEADS):
        h = hq // GROUP
        qh = qb[:, hq * HEAD_DIM:(hq + 1) * HEAD_DIM]
        kh = kb[:, h * HEAD_DIM:(h + 1) * HEAD_DIM]
        vh = vb[:, h * HEAD_DIM:(h + 1) * HEAD_DIM]
        s = jnp.where(causal, _dot_nt(qh, kh), NEG_INF)
        sink = sink_ref[hq]
        mx = jnp.maximum(jnp.max(s, axis=-1, keepdims=True), sink)
        p = jnp.exp(s - mx)
        den = jnp.sum(p, axis=-1, keepdims=True) + jnp.exp(sink - mx)
        o = _dot(p.astype(BF16), vh) / den
        attn = attn + _dot(o.astype(BF16), wap_ref[hq * HEAD_DIM:(hq + 1) * HEAD_DIM, :])

    glu_in = _dot(hn, wglu_ref[...]) + bglu_ref[...]
    glu = glu_in[:, :CONV_CH] * _sigmoid_2x(glu_in[:, CONV_CH:])
    glu_out[...] = glu
    gscr[0:CONV_HALO, :] = jnp.zeros((CONV_HALO, CONV_CH), F32)
    gscr[CONV_HALO:CONV_HALO + N_META, :] = glu
    c = jnp.broadcast_to(dwb_ref[...], (N_META, CONV_CH))
    off = CONV_HALO - (CONV_K - 1)
    for kk in range(CONV_K):
        c = c + gscr[off + kk:off + kk + N_META, :] * dww_ref[kk:kk + 1, :]
    c = _silu_2x(_layer_norm(c, lng_ref[...], lnb_ref[...]))
    conv = _dot(c.astype(BF16), wcp_ref[...]) + bcp_ref[...]

    gates = _sigmoid_2x(_dot(hn, wgate_ref[...]) + bgate_ref[...])
    merged = gates[:, :D_MODEL] * attn + gates[:, D_MODEL:] * conv
    mix = _dot(merged.astype(BF16), wout_ref[...])
    h2 = m + _rms(mix, gpost_ref[...])
    hn2 = _rms(h2, gffn_ref[...]).astype(BF16)
    u_out[...] = _dot(hn2, wup_ref[...])


def _mixer_kernel(x_ref, gpre_ref, wqT_ref, bqT_ref, wk_ref, bk_ref, wvT_ref, bvT_ref,
                  wglu_ref, bglu_ref, wgate_ref, bgate_ref, sink_ref, wap_ref, dww_ref,
                  dwb_ref, lng_ref, lnb_ref, wcp_ref, bcp_ref, wout_ref, gpost_ref,
                  rc_ref, rs1_ref, rs2_ref, cosT_ref, sinT_ref,
                  kmeta_ref, vmetaT_ref, glumeta_ref,
                  out_ref,
                  kbuf, vTbuf, gbuf, shbuf, qT_s, oT_s, attn_s, cbuf, bias_s):
    T = MIX_TILE
    nblk = T // BLOCK
    t = pl.program_id(1)

    @pl.when(t == 0)
    def _init():
        kbuf[:, 0:BLOCK, :] = jnp.zeros((4, BLOCK, LANES), BF16)
        vTbuf[:, 0:BLOCK] = jnp.zeros((KV_WIDTH, BLOCK), BF16)
        gbuf[0:CONV_HALO - N_META, :] = jnp.zeros((CONV_HALO - N_META, CONV_CH), F32)
        gbuf[CONV_HALO - N_META:CONV_HALO, :] = glumeta_ref[...]
        key = lax.broadcasted_iota(jnp.int32, (2 * BLOCK, 2 * BLOCK), 0)
        qry = lax.broadcasted_iota(jnp.int32, (2 * BLOCK, 2 * BLOCK), 1) % BLOCK
        vis = (key > qry) & (key <= qry + WINDOW)
        bias_s[0] = jnp.where(vis, 0.0, NEG_INF).astype(F32)
        bias_s[1] = jnp.where(vis & (key >= BLOCK), 0.0, NEG_INF).astype(F32)

    x = x_ref[0]
    hn = _rms(x, gpre_ref[...]).astype(BF16)

    glu_in = _dot(hn, wglu_ref[...]) + bglu_ref[...]
    gbuf[CONV_HALO:CONV_HALO + T, :] = glu_in[:, :CONV_CH] * _sigmoid_2x(glu_in[:, CONV_CH:])
    for s in range(1, CONV_SHIFTS):
        shbuf[s - 1] = gbuf[SUBLANES - s:SUBLANES - s + T + CONV_PAD, :]

    qT = _dot_nt(wqT_ref[...], hn) + bqT_ref[...]
    cosT, sinT = cosT_ref[...], sinT_ref[...]
    for hq in range(N_Q_HEADS):
        base = hq * HEAD_DIM
        r0 = qT[base:base + ROT_HALF]
        r1 = qT[base + ROT_HALF:base + ROT_DIM]
        rot = jnp.concatenate([r0 * cosT - r1 * sinT, r1 * cosT + r0 * sinT], axis=0)
        qT_s[base:base + ROT_DIM, :] = rot.astype(BF16)
        qT_s[base + ROT_DIM:base + HEAD_DIM, :] = qT[base + ROT_DIM:base + HEAD_DIM].astype(BF16)

    k = _dot(hn, wk_ref[...]) + bk_ref[...]
    k = _rope_rows(k, rc_ref[...], rs1_ref[...], rs2_ref[...])
    for i, kv in enumerate(_key_variants(k)):
        kbuf[i, BLOCK:BLOCK + T, :] = kv
    kmeta_var = _key_variants(kmeta_ref[...])
    vT = _dot_nt(wvT_ref[...], hn) + bvT_ref[...]
    vTbuf[:, BLOCK:BLOCK + T] = vT.astype(BF16)
    vmetaT = vmetaT_ref[...]

    lane2 = lax.broadcasted_iota(jnp.int32, (1, 2 * BLOCK), 1)
    first_sel = jnp.where(t == 0, 1, 0)

    items = [(j, gp, hl) for j in range(nblk) for gp in range(4) for hl in range(2)]

    def scores(item):
        j, gp, hl = item
        h = gp // 2
        c0 = j * BLOCK
        rhs = jnp.concatenate(
            [qT_s[256 * gp:256 * gp + 128, c0:c0 + BLOCK],
             qT_s[256 * gp + 128:256 * gp + 256, c0:c0 + BLOCK]], axis=1)
        s_loc = _dot(kbuf[2 * h + hl, c0:c0 + 2 * BLOCK, :], rhs)
        s_met = _dot(kmeta_var[2 * h + hl], rhs)
        return s_loc, s_met

    def softmax(item, s_loc, s_met):
        j, gp, hl = item
        ha = 4 * gp + hl
        hb = ha + 2
        s_loc = s_loc + (bias_s[first_sel] if j == 0 else bias_s[0])
        sink = jnp.where(lane2 < BLOCK, sink_ref[ha], sink_ref[hb])
        mx = jnp.maximum(jnp.maximum(jnp.max(s_loc, axis=0, keepdims=True),
                                     jnp.max(s_met, axis=0, keepdims=True)), sink)
        p_loc = jnp.exp(s_loc - mx)
        p_met = jnp.exp(s_met - mx)
        den = (jnp.sum(p_loc, axis=0, keepdims=True)
               + jnp.sum(p_met, axis=0, keepdims=True) + jnp.exp(sink - mx))
        return p_loc.astype(BF16), p_met.astype(BF16), 1.0 / den

    def weighted_values(item, p_loc, p_met, inv_den):
        j, gp, hl = item
        h = gp // 2
        c0 = j * BLOCK
        ha = 4 * gp + hl
        hb = ha + 2
        vT_h = vTbuf[h * HEAD_DIM:(h + 1) * HEAD_DIM, c0:c0 + 2 * BLOCK]
        vmT_h = vmetaT[h * HEAD_DIM:(h + 1) * HEAD_DIM, :]
        oT = (_dot(vT_h, p_loc) + _dot(vmT_h, p_met)) * inv_den
        oT_s[j, ha * HEAD_DIM:(ha + 1) * HEAD_DIM, :] = oT[:, :BLOCK]
        oT_s[j, hb * HEAD_DIM:(hb + 1) * HEAD_DIM, :] = oT[:, BLOCK:]

    def conv_block(cc, rc):
        cols = slice(cc * CONV_COLS, (cc + 1) * CONV_COLS)
        r0 = rc * CONV_ROWS
        nsub = CONV_ROWS // SUBLANES
        bias8 = jnp.broadcast_to(dwb_ref[:, cols], (SUBLANES, CONV_COLS))
        accs = [bias8] * nsub
        for d in range(CONV_K):
            a, s = divmod(d, SUBLANES)
            w = dww_ref[CONV_K - 1 - d, :, cols]
            for r in range(nsub):
                if s == 0:
                    lo = r0 + r * SUBLANES + CONV_HALO - SUBLANES * a
                    g = gbuf[lo:lo + SUBLANES, cols]
                else:
                    lo = r0 + r * SUBLANES + CONV_PAD - SUBLANES * a
                    g = shbuf[s - 1, lo:lo + SUBLANES, cols]
                accs[r] = accs[r] + g * w
        for r in range(nsub):
            cbuf[r0 + r * SUBLANES:r0 + (r + 1) * SUBLANES, cols] = accs[r]

    conv_blocks = [(cc, rc) for cc in range(CONV_CH // CONV_COLS) for rc in range(T // CONV_ROWS)]
    conv_per_item = -(-len(conv_blocks) // len(items))

    n_items = len(items)
    items_per_blk = n_items // nblk
    items_per_gate = n_items // GATE_CHUNKS
    gate_w = 2 * D_MODEL // GATE_CHUNKS
    pending = [scores(it) for it in items[:ATTN_LOOKAHEAD]]
    probs = []
    gate_parts = []
    for step in range(n_items + PV_DELAY + ATTN_PROJ_DELAY):
        if step + ATTN_LOOKAHEAD < n_items:
            pending.append(scores(items[step + ATTN_LOOKAHEAD]))
        if step < n_items:
            probs.append(softmax(items[step], *pending.pop(0)))
        i = step - PV_DELAY
        if 0 <= i < n_items:
            weighted_values(items[i], *probs[i])
            probs[i] = None
        for _ in range(conv_per_item):
            if conv_blocks:
                conv_block(*conv_blocks.pop(0))
        if step < n_items and (step + 1) % items_per_gate == 0:
            gcols = slice(step // items_per_gate * gate_w, (step // items_per_gate + 1) * gate_w)
            gate_parts.append(_sigmoid_2x(_dot(hn, wgate_ref[:, gcols]) + bgate_ref[:, gcols]))
        i = step - PV_DELAY - ATTN_PROJ_DELAY
        if 0 <= i < n_items and (i + 1) % items_per_blk == 0:
            j = i // items_per_blk
            o_blk = oT_s[j].T.astype(BF16)
            attn_s[j * BLOCK:(j + 1) * BLOCK, :] = _dot(o_blk, wap_ref[...])
    while conv_blocks:
        conv_block(*conv_blocks.pop(0))
    gates = jnp.concatenate(gate_parts, axis=1)

    kbuf[:, 0:BLOCK, :] = kbuf[:, T:T + BLOCK, :]
    vTbuf[:, 0:BLOCK] = vTbuf[:, T:T + BLOCK]
    gbuf[0:CONV_HALO, :] = gbuf[T:T + CONV_HALO, :]

    c = _silu_2x(_layer_norm(cbuf[...], lng_ref[...], lnb_ref[...]))
    conv = _dot(c.astype(BF16), wcp_ref[...]) + bcp_ref[...]

    merged = gates[:, :D_MODEL] * attn_s[...] + gates[:, D_MODEL:] * conv
    mix = _dot(merged.astype(BF16), wout_ref[...])
    out_ref[0] = x + _rms(mix, gpost_ref[...])


def _ffn_cols(cidx, half):
    start = half * FFN_DIM + cidx * FFN_CHUNK
    return slice(start, start + FFN_CHUNK)


def _ffn_kernel(h_ref, gpre_ref, gpost_ref, wup_ref, dwp_ref, wdown_ref, umeta_ref,
                out_ref, hn_s, ubuf, carry, act_s, acc_s):
    T = FFN_TILE
    t = pl.program_id(1)

    @pl.when(t == 0)
    def _init():
        carry[...] = umeta_ref[...]

    h = h_ref[0]
    hn_s[...] = _rms(h, gpre_ref[...]).astype(BF16)

    def up(cidx):
        par = cidx % (FFN_UP_AHEAD + 1)
        for half in range(2):
            cols = _ffn_cols(cidx, half)
            u = _dot(hn_s[...], wup_ref[:, cols])
            ubuf[par, half, 0:SUBLANES, :] = carry[:, cols]
            ubuf[par, half, SUBLANES:SUBLANES + T, :] = u
            carry[:, cols] = u[T - SUBLANES:T, :]

    def activate(cidx):
        par = cidx % (FFN_UP_AHEAD + 1)
        ys = []
        for half in range(2):
            w = dwp_ref[:, _ffn_cols(cidx, half)]
            ue = ubuf[par, half]
            delayed = [pltpu.roll(ue, d, 0)[SUBLANES:SUBLANES + T, :] if d else ue[SUBLANES:SUBLANES + T, :]
                       for d in range(FFN_CONV_K)]
            ys.append(w[3:4, :] + w[0:1, :] * delayed[2] + w[1:2, :] * delayed[1]
                      + w[2:3, :] * delayed[0])
        act_s[:, cidx * FFN_CHUNK:(cidx + 1) * FFN_CHUNK] = (_silu_2x(ys[0]) * ys[1]).astype(BF16)

    total = None
    for cidx in range(FFN_UP_AHEAD):
        up(cidx)
    for cidx in range(N_FFN_CHUNKS):
        if cidx + FFN_UP_AHEAD < N_FFN_CHUNKS:
            up(cidx + FFN_UP_AHEAD)
        activate(cidx)
        for lo, hi in FFN_DOWN_GROUPS:
            if cidx == hi - 1:
                part = _dot(act_s[:, lo * FFN_CHUNK:hi * FFN_CHUNK],
                            wdown_ref[lo * FFN_CHUNK:hi * FFN_CHUNK, :])
                if hi == N_FFN_CHUNKS:
                    total = acc_s[...] + part
                elif lo == 0:
                    acc_s[...] = part
                else:
                    acc_s[...] += part
    out_ref[0] = h + _rms(total, gpost_ref[...])


def _const_spec(shape):
    nd = len(shape)
    return pl.BlockSpec(shape, lambda *_: (0,) * nd, pipeline_mode=pl.Buffered(1))


def _rope_tables(n_pos):
    inv_freq = ROPE_THETA ** (-jnp.arange(ROT_HALF, dtype=F32) * 2.0 / ROT_DIM)
    ang = jnp.arange(n_pos).astype(F32)[:, None] * inv_freq[None, :]
    cos, sin = jnp.cos(ang), jnp.sin(ang)
    ones = jnp.ones((n_pos, HEAD_DIM - ROT_DIM), F32)
    zeros = jnp.zeros((n_pos, HEAD_DIM - ROT_HALF), F32)
    c = jnp.concatenate([cos, cos, ones], axis=1)
    s1 = jnp.concatenate([-sin, zeros], axis=1)
    s2 = jnp.concatenate([zeros[:, :ROT_HALF], sin, zeros[:, :HEAD_DIM - ROT_DIM]], axis=1)
    rep = LANES // HEAD_DIM
    return (jnp.tile(c, (1, rep)), jnp.tile(s1, (1, rep)), jnp.tile(s2, (1, rep)),
            cos.T, sin.T)


def kernel(x, meta_tokens, norm_pre_mix, norm_post_mix, w_in, b_in, attn_sinks, w_attn_proj,
           conv_dw_w, conv_dw_b, conv_ln_g, conv_ln_b, w_conv_proj, b_conv_proj, w_out,
           norm_pre_ffn, norm_post_ffn, w_up, ffn_dw_w, ffn_dw_b, w_down):
    bsz, seq, _ = x.shape
    assert seq % MIX_TILE == 0 and seq % FFN_TILE == 0
    scale = HEAD_DIM ** -0.5
    row = lambda v: v.reshape(1, -1).astype(F32)

    wi, bi = w_in[0], b_in[0]
    c_q, c_k, c_v, c_glu = ATTN_WIDTH, ATTN_WIDTH + KV_WIDTH, ATTN_WIDTH + 2 * KV_WIDTH, \
        ATTN_WIDTH + 2 * KV_WIDTH + 2 * CONV_CH
    wqT = (wi[:, :c_q] * scale).T.astype(BF16)
    bq = bi[:c_q] * scale
    wk = wi[:, c_q:c_k].astype(BF16)
    bk = row(bi[c_q:c_k])
    wvT = wi[:, c_k:c_v].T.astype(BF16)
    bv = bi[c_k:c_v]
    glu_scale = jnp.concatenate([jnp.ones((CONV_CH,), F32), jnp.full((CONV_CH,), 0.5, F32)])
    wglu = (wi[:, c_v:c_glu] * glu_scale).astype(BF16)
    bglu = row(bi[c_v:c_glu] * glu_scale)
    wgate = (wi[:, c_glu:] * 0.5).astype(BF16)
    bgate = row(bi[c_glu:] * 0.5)
    sinks = attn_sinks[0].astype(F32)
    wap = w_attn_proj[0].astype(BF16)
    dww = conv_dw_w[0].astype(F32)
    dww8 = jnp.broadcast_to(dww[:, None, :], (CONV_K, SUBLANES, CONV_CH))
    dwb = row(conv_dw_b[0])
    lng, lnb = row(conv_ln_g[0] * 0.5), row(conv_ln_b[0] * 0.5)
    wcp = w_conv_proj[0].astype(BF16)
    bcp = row(b_conv_proj[0])
    wout = w_out[0].astype(BF16)
    gpre, gpost = row(norm_pre_mix[0]), row(norm_post_mix[0])
    gffn, gffn_post = row(norm_pre_ffn[0]), row(norm_post_ffn[0])

    nch = 2 * N_FFN_CHUNKS
    wup = w_up[0].astype(BF16)
    dwp = jnp.concatenate(
        [ffn_dw_w[0], ffn_dw_b[0][None, :], jnp.zeros((SUBLANES - FFN_CONV_K - 1, 2 * FFN_DIM), F32)],
        axis=0)
    dwp = dwp * jnp.concatenate([jnp.full((FFN_DIM,), 0.5, F32), jnp.ones((FFN_DIM,), F32)])
    wdown = w_down[0].astype(BF16)

    rc, rs1, rs2, cosT, sinT = _rope_tables(N_META + seq)

    smem = pl.BlockSpec(memory_space=pltpu.SMEM)
    vmem = pl.BlockSpec(memory_space=pltpu.VMEM)

    k_meta, v_meta, glu_meta, u_meta = pl.pallas_call(
        _meta_kernel,
        out_shape=(jax.ShapeDtypeStruct((N_META, KV_WIDTH), F32),
                   jax.ShapeDtypeStruct((N_META, KV_WIDTH), F32),
                   jax.ShapeDtypeStruct((N_META, CONV_CH), F32),
                   jax.ShapeDtypeStruct((N_META, 2 * FFN_DIM), F32)),
        in_specs=[vmem] * 12 + [smem] + [vmem] * 14,
        out_specs=(vmem, vmem, vmem, vmem),
        scratch_shapes=[pltpu.VMEM((CONV_HALO + N_META, CONV_CH), F32)],
        compiler_params=pltpu.CompilerParams(vmem_limit_bytes=VMEM_LIMIT),
        name="meta_prologue",
    )(meta_tokens.astype(F32), gpre, wqT, row(bq), wk, bk, wvT, row(bv), wglu, bglu, wgate,
      bgate, sinks, wap, dww, dwb, lng, lnb, wcp, bcp, wout, gpost, gffn, wup,
      rc[:N_META], rs1[:N_META], rs2[:N_META])

    vmetaT = v_meta.T.astype(BF16)
    umeta_tail = u_meta[N_META - SUBLANES:, :]

    T = MIX_TILE
    bqT = jnp.broadcast_to(bq[:, None], (ATTN_WIDTH, T)).astype(F32)
    bvT = jnp.broadcast_to(bv[:, None], (KV_WIDTH, T)).astype(F32)
    tile_spec = pl.BlockSpec((1, T, D_MODEL), lambda b, t: (b, t, 0))
    rope_spec = pl.BlockSpec((T, LANES), lambda b, t: (t, 0))
    ropeT_spec = pl.BlockSpec((SUBLANES, T), lambda b, t: (0, t))
    mixer_in = [
        (x, tile_spec), (gpre, None), (wqT, None), (bqT, None), (wk, None), (bk, None),
        (wvT, None), (bvT, None), (wglu, None), (bglu, None), (wgate, None), (bgate, None),
        (sinks, smem), (wap, None), (dww8, None), (dwb, None), (lng, None), (lnb, None),
        (wcp, None), (bcp, None), (wout, None), (gpost, None),
        (rc[N_META:], rope_spec), (rs1[N_META:], rope_spec), (rs2[N_META:], rope_spec),
        (cosT[:, N_META:], ropeT_spec), (sinT[:, N_META:], ropeT_spec),
        (k_meta, None), (vmetaT, None), (glu_meta, None),
    ]
    h2 = pl.pallas_call(
        _mixer_kernel,
        out_shape=jax.ShapeDtypeStruct((bsz, seq, D_MODEL), F32),
        grid=(bsz, seq // T),
        in_specs=[spec if spec is not None else _const_spec(a.shape) for a, spec in mixer_in],
        out_specs=tile_spec,
        scratch_shapes=[
            pltpu.VMEM((4, BLOCK + T, LANES), BF16),
            pltpu.VMEM((KV_WIDTH, BLOCK + T), BF16),
            pltpu.VMEM((CONV_HALO + T, CONV_CH), F32),
            pltpu.VMEM((CONV_SHIFTS - 1, T + CONV_PAD, CONV_CH), F32),
            pltpu.VMEM((ATTN_WIDTH, T), BF16),
            pltpu.VMEM((T // BLOCK, ATTN_WIDTH, BLOCK), F32),
            pltpu.VMEM((T, D_MODEL), F32),
            pltpu.VMEM((T, CONV_CH), F32),
            pltpu.VMEM((2, 2 * BLOCK, 2 * BLOCK), F32),
        ],
        compiler_params=pltpu.CompilerParams(
            dimension_semantics=("arbitrary", "arbitrary"), vmem_limit_bytes=VMEM_LIMIT),
        name="mixer",
    )(*[a for a, _ in mixer_in])

    T2 = FFN_TILE
    tile2 = pl.BlockSpec((1, T2, D_MODEL), lambda b, t: (b, t, 0))
    ffn_in = [(h2, tile2), (gffn, None), (gffn_post, None), (wup, None), (dwp, None),
              (wdown, None), (umeta_tail, None)]
    out = pl.pallas_call(
        _ffn_kernel,
        out_shape=jax.ShapeDtypeStruct((bsz, seq, D_MODEL), F32),
        grid=(bsz, seq // T2),
        in_specs=[spec if spec is not None else _const_spec(a.shape) for a, spec in ffn_in],
        out_specs=tile2,
        scratch_shapes=[
            pltpu.VMEM((T2, D_MODEL), BF16),
            pltpu.VMEM((FFN_UP_AHEAD + 1, 2, SUBLANES + T2, FFN_CHUNK), F32),
            pltpu.VMEM((SUBLANES, 2 * FFN_DIM), F32),
            pltpu.VMEM((T2, FFN_DIM), BF16),
            pltpu.VMEM((T2, D_MODEL), F32),
        ],
        compiler_params=pltpu.CompilerParams(
            dimension_semantics=("arbitrary", "arbitrary"), vmem_limit_bytes=VMEM_LIMIT),
        name="conv_ffn",
    )(*[a for a, _ in ffn_in])
    return out.astype(x.dtype)
```

```python
import functools

import numpy as np
import jax
import jax.numpy as jnp
from jax import lax
from jax.experimental import pallas as pl
from jax.experimental.pallas import tpu as pltpu

D_MODEL = 1024
N_META = 16
N_Q_HEADS = 16
N_KV_HEADS = 2
HEAD_DIM = 64
GROUP = N_Q_HEADS // N_KV_HEADS
ROT_DIM = HEAD_DIM // 4
ROT_HALF = ROT_DIM // 2
ROPE_THETA = 500000.0
WINDOW = 128
BLOCK = 128
ATTN_WIDTH = N_Q_HEADS * HEAD_DIM
KV_WIDTH = N_KV_HEADS * HEAD_DIM
CONV_CH = D_MODEL
CONV_K = 31
FFN_DIM = 2816
FFN_CONV_K = 3
RMS_EPS = 1e-6
LN_EPS = 1e-5
NEG_INF = -1e30

LANES = 128
SUBLANES = 8
MIX_TILE = 512
FFN_TILE = 512
FFN_CHUNK = 256
N_FFN_CHUNKS = FFN_DIM // FFN_CHUNK
FFN_DOWN_GROUPS = ((0, 3), (3, 6), (6, 9), (9, 11))
FFN_UP_AHEAD = 2
CONV_HALO = 32
CONV_SHIFTS = SUBLANES
CONV_PAD = CONV_HALO - SUBLANES
CONV_ROWS = 32
CONV_COLS = 256
ATTN_LOOKAHEAD = 2
GATE_CHUNKS = 4
PV_DELAY = 0
ATTN_PROJ_DELAY = 0
VMEM_LIMIT = 56 * 1024 * 1024

F32 = jnp.float32
BF16 = jnp.bfloat16

_NT = (((1,), (1,)), ((), ()))


def _dot(a, b):
    return jnp.dot(a, b, preferred_element_type=F32)


def _dot_nt(a, b):
    return lax.dot_general(a, b, _NT, preferred_element_type=F32)


def _rms(x, g):
    ms = jnp.mean(x * x, axis=-1, keepdims=True)
    return x * lax.rsqrt(ms + RMS_EPS) * g


def _layer_norm(x, g, b):
    mu = jnp.mean(x, axis=-1, keepdims=True)
    xc = x - mu
    var = jnp.mean(xc * xc, axis=-1, keepdims=True)
    return xc * lax.rsqrt(var + LN_EPS) * g + b


def _sigmoid(x):
    return 0.5 * jnp.tanh(0.5 * x) + 0.5


def _silu_2x(hx):
    return hx * jnp.tanh(hx) + hx


def _silu(x):
    return _silu_2x(0.5 * x)


def _rope_rows(z, c, s1, s2):
    parts = []
    for g in range(z.shape[1] // LANES):
        zg = z[:, g * LANES:(g + 1) * LANES]
        parts.append(zg * c + pltpu.roll(zg, LANES - ROT_HALF, 1) * s1
                     + pltpu.roll(zg, ROT_HALF, 1) * s2)
    return parts[0] if len(parts) == 1 else jnp.concatenate(parts, axis=1)


def _key_variants(k):
    lane = lax.broadcasted_iota(jnp.int32, k.shape, 1)
    lo = lane < HEAD_DIM
    kr = pltpu.roll(k, HEAD_DIM, 1)
    zero = jnp.zeros_like(k)
    return [jnp.where(lo, k, zero).astype(BF16), jnp.where(lo, zero, kr).astype(BF16),
            jnp.where(lo, kr, zero).astype(BF16), jnp.where(lo, zero, k).astype(BF16)]


def _meta_kernel(meta_ref, gpre_ref, wqT_ref, bq_ref, wk_ref, bk_ref, wvT_ref, bv_ref,
                 wglu_ref, bglu_ref, wgate_ref, bgate_ref, sink_ref, wap_ref, dww_ref,
                 dwb_ref, lng_ref, lnb_ref, wcp_ref, bcp_ref, wout_ref, gpost_ref,
                 gffn_ref, wup_ref, rc_ref, rs1_ref, rs2_ref,
                 k_out, v_out, glu_out, u_out, gscr):
    m = meta_ref[...]
    hn = _rms(m, gpre_ref[...]).astype(BF16)
    q = _dot_nt(hn, wqT_ref[...]) + bq_ref[...]
    k = _dot(hn, wk_ref[...]) + bk_ref[...]
    v = _dot_nt(hn, wvT_ref[...]) + bv_ref[...]
    rc, rs1, rs2 = rc_ref[...], rs1_ref[...], rs2_ref[...]
    q = _rope_rows(q, rc, rs1, rs2)
    k = _rope_rows(k, rc, rs1, rs2)
    k_out[...] = k
    v_out[...] = v

    row = lax.broadcasted_iota(jnp.int32, (N_META, N_META), 0)
    col = lax.broadcasted_iota(jnp.int32, (N_META, N_META), 1)
    causal = col <= row
    qb, kb, vb = q.astype(BF16), k.astype(BF16), v.astype(BF16)
    attn = jnp.zeros((N_META, D_MODEL), F32)
    for hq in range(N_Q_HEADS):
        h = hq // GROUP
        qh = qb[:, hq * HEAD_DIM:(hq + 1) * HEAD_DIM]
        kh = kb[:, h * HEAD_DIM:(h + 1) * HEAD_DIM]
        vh = vb[:, h * HEAD_DIM:(h + 1) * HEAD_DIM]
        s = jnp.where(causal, _dot_nt(qh, kh), NEG_INF)
        sink = sink_ref[hq]
        mx = jnp.maximum(jnp.max(s, axis=-1, keepdims=True), sink)
        p = jnp.exp(s - mx)
        den = jnp.sum(p, axis=-1, keepdims=True) + jnp.exp(sink - mx)
        o = _dot(p.astype(BF16), vh) / den
        attn = attn + _dot(o.astype(BF16), wap_ref[hq * HEAD_DIM:(hq + 1) * HEAD_DIM, :])

    glu_in = _dot(hn, wglu_ref[...]) + bglu_ref[...]
    glu = glu_in[:, :CONV_CH] * _sigmoid(glu_in[:, CONV_CH:])
    glu_out[...] = glu
    gscr[0:CONV_HALO, :] = jnp.zeros((CONV_HALO, CONV_CH), F32)
    gscr[CONV_HALO:CONV_HALO + N_META, :] = glu
    c = jnp.broadcast_to(dwb_ref[...], (N_META, CONV_CH))
    off = CONV_HALO - (CONV_K - 1)
    for kk in range(CONV_K):
        c = c + gscr[off + kk:off + kk + N_META, :] * dww_ref[kk:kk + 1, :]
    c = _silu(_layer_norm(c, lng_ref[...], lnb_ref[...]))
    conv = _dot(c.astype(BF16), wcp_ref[...]) + bcp_ref[...]

    gates = _sigmoid(_dot(hn, wgate_ref[...]) + bgate_ref[...])
    merged = gates[:, :D_MODEL] * attn + gates[:, D_MODEL:] * conv
    mix = _dot(merged.astype(BF16), wout_ref[...])
    h2 = m + _rms(mix, gpost_ref[...])
    hn2 = _rms(h2, gffn_ref[...]).astype(BF16)
    u_out[...] = _dot(hn2, wup_ref[...])


def _mixer_kernel(x_ref, gpre_ref, wqT_ref, bqT_ref, wk_ref, bk_ref, wvT_ref, bvT_ref,
                  wglu_ref, bglu_ref, wgate_ref, bgate_ref, sink_ref, wap_ref, dww_ref,
                  dwb_ref, lng_ref, lnb_ref, wcp_ref, bcp_ref, wout_ref, gpost_ref,
                  rc_ref, rs1_ref, rs2_ref, cosT_ref, sinT_ref,
                  kmeta_ref, vmetaT_ref, glumeta_ref,
                  out_ref,
                  kbuf, vTbuf, gbuf, shbuf, qT_s, oT_s, attn_s, cbuf, bias_s):
    T = MIX_TILE
    nblk = T // BLOCK
    t = pl.program_id(1)

    @pl.when(t == 0)
    def _init():
        kbuf[:, 0:BLOCK, :] = jnp.zeros((4, BLOCK, LANES), BF16)
        vTbuf[:, 0:BLOCK] = jnp.zeros((KV_WIDTH, BLOCK), BF16)
        gbuf[0:CONV_HALO - N_META, :] = jnp.zeros((CONV_HALO - N_META, CONV_CH), F32)
        gbuf[CONV_HALO - N_META:CONV_HALO, :] = glumeta_ref[...]
        key = lax.broadcasted_iota(jnp.int32, (2 * BLOCK, 2 * BLOCK), 0)
        qry = lax.broadcasted_iota(jnp.int32, (2 * BLOCK, 2 * BLOCK), 1) % BLOCK
        vis = (key > qry) & (key <= qry + WINDOW)
        bias_s[0] = jnp.where(vis, 0.0, NEG_INF).astype(F32)
        bias_s[1] = jnp.where(vis & (key >= BLOCK), 0.0, NEG_INF).astype(F32)

    x = x_ref[0]
    hn = _rms(x, gpre_ref[...]).astype(BF16)

    glu_in = _dot(hn, wglu_ref[...]) + bglu_ref[...]
    gbuf[CONV_HALO:CONV_HALO + T, :] = glu_in[:, :CONV_CH] * _sigmoid(glu_in[:, CONV_CH:])
    for s in range(1, CONV_SHIFTS):
        shbuf[s - 1] = gbuf[SUBLANES - s:SUBLANES - s + T + CONV_PAD, :]

    qT = _dot_nt(wqT_ref[...], hn) + bqT_ref[...]
    cosT, sinT = cosT_ref[...], sinT_ref[...]
    for hq in range(N_Q_HEADS):
        base = hq * HEAD_DIM
        r0 = qT[base:base + ROT_HALF]
        r1 = qT[base + ROT_HALF:base + ROT_DIM]
        rot = jnp.concatenate([r0 * cosT - r1 * sinT, r1 * cosT + r0 * sinT], axis=0)
        qT_s[base:base + ROT_DIM, :] = rot.astype(BF16)
        qT_s[base + ROT_DIM:base + HEAD_DIM, :] = qT[base + ROT_DIM:base + HEAD_DIM].astype(BF16)

    k = _dot(hn, wk_ref[...]) + bk_ref[...]
    k = _rope_rows(k, rc_ref[...], rs1_ref[...], rs2_ref[...])
    for i, kv in enumerate(_key_variants(k)):
        kbuf[i, BLOCK:BLOCK + T, :] = kv
    kmeta_var = _key_variants(kmeta_ref[...])
    vT = _dot_nt(wvT_ref[...], hn) + bvT_ref[...]
    vTbuf[:, BLOCK:BLOCK + T] = vT.astype(BF16)
    vmetaT = vmetaT_ref[...]

    lane2 = lax.broadcasted_iota(jnp.int32, (1, 2 * BLOCK), 1)
    first_sel = jnp.where(t == 0, 1, 0)

    items = [(j, gp, hl) for j in range(nblk) for gp in range(4) for hl in range(2)]

    def scores(item):
        j, gp, hl = item
        h = gp // 2
        c0 = j * BLOCK
        rhs = jnp.concatenate(
            [qT_s[256 * gp:256 * gp + 128, c0:c0 + BLOCK],
             qT_s[256 * gp + 128:256 * gp + 256, c0:c0 + BLOCK]], axis=1)
        s_loc = _dot(kbuf[2 * h + hl, c0:c0 + 2 * BLOCK, :], rhs)
        s_met = _dot(kmeta_var[2 * h + hl], rhs)
        return s_loc, s_met

    def softmax(item, s_loc, s_met):
        j, gp, hl = item
        ha = 4 * gp + hl
        hb = ha + 2
        s_loc = s_loc + (bias_s[first_sel] if j == 0 else bias_s[0])
        sink = jnp.where(lane2 < BLOCK, sink_ref[ha], sink_ref[hb])
        mx = jnp.maximum(jnp.maximum(jnp.max(s_loc, axis=0, keepdims=True),
                                     jnp.max(s_met, axis=0, keepdims=True)), sink)
        p_loc = jnp.exp(s_loc - mx)
        p_met = jnp.exp(s_met - mx)
        den = (jnp.sum(p_loc, axis=0, keepdims=True)
               + jnp.sum(p_met, axis=0, keepdims=True) + jnp.exp(sink - mx))
        return p_loc.astype(BF16), p_met.astype(BF16), 1.0 / den

    def weighted_values(item, p_loc, p_met, inv_den):
        j, gp, hl = item
        h = gp // 2
        c0 = j * BLOCK
        ha = 4 * gp + hl
        hb = ha + 2
        vT_h = vTbuf[h * HEAD_DIM:(h + 1) * HEAD_DIM, c0:c0 + 2 * BLOCK]
        vmT_h = vmetaT[h * HEAD_DIM:(h + 1) * HEAD_DIM, :]
        oT = (_dot(vT_h, p_loc) + _dot(vmT_h, p_met)) * inv_den
        oT_s[j, ha * HEAD_DIM:(ha + 1) * HEAD_DIM, :] = oT[:, :BLOCK]
        oT_s[j, hb * HEAD_DIM:(hb + 1) * HEAD_DIM, :] = oT[:, BLOCK:]

    def conv_block(cc, rc):
        cols = slice(cc * CONV_COLS, (cc + 1) * CONV_COLS)
        r0 = rc * CONV_ROWS
        nsub = CONV_ROWS // SUBLANES
        bias8 = jnp.broadcast_to(dwb_ref[:, cols], (SUBLANES, CONV_COLS))
        accs = [bias8] * nsub
        for d in range(CONV_K):
            a, s = divmod(d, SUBLANES)
            w = dww_ref[CONV_K - 1 - d, :, cols]
            for r in range(nsub):
                if s == 0:
                    lo = r0 + r * SUBLANES + CONV_HALO - SUBLANES * a
                    g = gbuf[lo:lo + SUBLANES, cols]
                else:
                    lo = r0 + r * SUBLANES + CONV_PAD - SUBLANES * a
                    g = shbuf[s - 1, lo:lo + SUBLANES, cols]
                accs[r] = accs[r] + g * w
        for r in range(nsub):
            cbuf[r0 + r * SUBLANES:r0 + (r + 1) * SUBLANES, cols] = accs[r]

    conv_blocks = [(cc, rc) for cc in range(CONV_CH // CONV_COLS) for rc in range(T // CONV_ROWS)]
    conv_per_item = -(-len(conv_blocks) // len(items))

    n_items = len(items)
    items_per_blk = n_items // nblk
    items_per_gate = n_items // GATE_CHUNKS
    gate_w = 2 * D_MODEL // GATE_CHUNKS
    pending = [scores(it) for it in items[:ATTN_LOOKAHEAD]]
    probs = []
    gate_parts = []
    for step in range(n_items + PV_DELAY + ATTN_PROJ_DELAY):
        if step + ATTN_LOOKAHEAD < n_items:
            pending.append(scores(items[step + ATTN_LOOKAHEAD]))
        if step < n_items:
            probs.append(softmax(items[step], *pending.pop(0)))
        i = step - PV_DELAY
        if 0 <= i < n_items:
            weighted_values(items[i], *probs[i])
            probs[i] = None
        for _ in range(conv_per_item):
            if conv_blocks:
                conv_block(*conv_blocks.pop(0))
        if step < n_items and (step + 1) % items_per_gate == 0:
            gcols = slice(step // items_per_gate * gate_w, (step // items_per_gate + 1) * gate_w)
            gate_parts.append(_sigmoid(_dot(hn, wgate_ref[:, gcols]) + bgate_ref[:, gcols]))
        i = step - PV_DELAY - ATTN_PROJ_DELAY
        if 0 <= i < n_items and (i + 1) % items_per_blk == 0:
            j = i // items_per_blk
            o_blk = oT_s[j].T.astype(BF16)
            attn_s[j * BLOCK:(j + 1) * BLOCK, :] = _dot(o_blk, wap_ref[...])
    while conv_blocks:
        conv_block(*conv_blocks.pop(0))
    gates = jnp.concatenate(gate_parts, axis=1)

    kbuf[:, 0:BLOCK, :] = kbuf[:, T:T + BLOCK, :]
    vTbuf[:, 0:BLOCK] = vTbuf[:, T:T + BLOCK]
    gbuf[0:CONV_HALO, :] = gbuf[T:T + CONV_HALO, :]

    c = _silu(_layer_norm(cbuf[...], lng_ref[...], lnb_ref[...]))
    conv = _dot(c.astype(BF16), wcp_ref[...]) + bcp_ref[...]

    merged = gates[:, :D_MODEL] * attn_s[...] + gates[:, D_MODEL:] * conv
    mix = _dot(merged.astype(BF16), wout_ref[...])
    out_ref[0] = x + _rms(mix, gpost_ref[...])


def _ffn_cols(cidx, half):
    start = half * FFN_DIM + cidx * FFN_CHUNK
    return slice(start, start + FFN_CHUNK)


def _ffn_kernel(h_ref, gpre_ref, gpost_ref, wup_ref, dwp_ref, wdown_ref, umeta_ref,
                out_ref, hn_s, ubuf, carry, act_s, acc_s):
    T = FFN_TILE
    t = pl.program_id(1)

    @pl.when(t == 0)
    def _init():
        carry[...] = umeta_ref[...]

    h = h_ref[0]
    hn_s[...] = _rms(h, gpre_ref[...]).astype(BF16)

    def up(cidx):
        par = cidx % (FFN_UP_AHEAD + 1)
        for half in range(2):
            cols = _ffn_cols(cidx, half)
            u = _dot(hn_s[...], wup_ref[:, cols])
            ubuf[par, half, 0:SUBLANES, :] = carry[:, cols]
            ubuf[par, half, SUBLANES:SUBLANES + T, :] = u
            carry[:, cols] = u[T - SUBLANES:T, :]

    def activate(cidx):
        par = cidx % (FFN_UP_AHEAD + 1)
        ys = []
        for half in range(2):
            w = dwp_ref[:, _ffn_cols(cidx, half)]
            ue = ubuf[par, half]
            delayed = [pltpu.roll(ue, d, 0)[SUBLANES:SUBLANES + T, :] if d else ue[SUBLANES:SUBLANES + T, :]
                       for d in range(FFN_CONV_K)]
            ys.append(w[3:4, :] + w[0:1, :] * delayed[2] + w[1:2, :] * delayed[1]
                      + w[2:3, :] * delayed[0])
        act_s[:, cidx * FFN_CHUNK:(cidx + 1) * FFN_CHUNK] = (_silu_2x(ys[0]) * ys[1]).astype(BF16)

    total = None
    for cidx in range(FFN_UP_AHEAD):
        up(cidx)
    for cidx in range(N_FFN_CHUNKS):
        if cidx + FFN_UP_AHEAD < N_FFN_CHUNKS:
            up(cidx + FFN_UP_AHEAD)
        activate(cidx)
        for lo, hi in FFN_DOWN_GROUPS:
            if cidx == hi - 1:
                part = _dot(act_s[:, lo * FFN_CHUNK:hi * FFN_CHUNK],
                            wdown_ref[lo * FFN_CHUNK:hi * FFN_CHUNK, :])
                if hi == N_FFN_CHUNKS:
                    total = acc_s[...] + part
                elif lo == 0:
                    acc_s[...] = part
                else:
                    acc_s[...] += part
    out_ref[0] = h + _rms(total, gpost_ref[...])


def _const_spec(shape):
    nd = len(shape)
    return pl.BlockSpec(shape, lambda *_: (0,) * nd, pipeline_mode=pl.Buffered(1))


def _rope_tables(n_pos):
    inv_freq = ROPE_THETA ** (-jnp.arange(ROT_HALF, dtype=F32) * 2.0 / ROT_DIM)
    ang = jnp.arange(n_pos).astype(F32)[:, None] * inv_freq[None, :]
    cos, sin = jnp.cos(ang), jnp.sin(ang)
    ones = jnp.ones((n_pos, HEAD_DIM - ROT_DIM), F32)
    zeros = jnp.zeros((n_pos, HEAD_DIM - ROT_HALF), F32)
    c = jnp.concatenate([cos, cos, ones], axis=1)
    s1 = jnp.concatenate([-sin, zeros], axis=1)
    s2 = jnp.concatenate([zeros[:, :ROT_HALF], sin, zeros[:, :HEAD_DIM - ROT_DIM]], axis=1)
    rep = LANES // HEAD_DIM
    return (jnp.tile(c, (1, rep)), jnp.tile(s1, (1, rep)), jnp.tile(s2, (1, rep)),
            cos.T, sin.T)


def kernel(x, meta_tokens, norm_pre_mix, norm_post_mix, w_in, b_in, attn_sinks, w_attn_proj,
           conv_dw_w, conv_dw_b, conv_ln_g, conv_ln_b, w_conv_proj, b_conv_proj, w_out,
           norm_pre_ffn, norm_post_ffn, w_up, ffn_dw_w, ffn_dw_b, w_down):
    bsz, seq, _ = x.shape
    assert seq % MIX_TILE == 0 and seq % FFN_TILE == 0
    scale = HEAD_DIM ** -0.5
    row = lambda v: v.reshape(1, -1).astype(F32)

    wi, bi = w_in[0], b_in[0]
    c_q, c_k, c_v, c_glu = ATTN_WIDTH, ATTN_WIDTH + KV_WIDTH, ATTN_WIDTH + 2 * KV_WIDTH, \
        ATTN_WIDTH + 2 * KV_WIDTH + 2 * CONV_CH
    wqT = (wi[:, :c_q] * scale).T.astype(BF16)
    bq = bi[:c_q] * scale
    wk = wi[:, c_q:c_k].astype(BF16)
    bk = row(bi[c_q:c_k])
    wvT = wi[:, c_k:c_v].T.astype(BF16)
    bv = bi[c_k:c_v]
    wglu = wi[:, c_v:c_glu].astype(BF16)
    bglu = row(bi[c_v:c_glu])
    wgate = wi[:, c_glu:].astype(BF16)
    bgate = row(bi[c_glu:])
    sinks = attn_sinks[0].astype(F32)
    wap = w_attn_proj[0].astype(BF16)
    dww = conv_dw_w[0].astype(F32)
    dww8 = jnp.broadcast_to(dww[:, None, :], (CONV_K, SUBLANES, CONV_CH))
    dwb = row(conv_dw_b[0])
    lng, lnb = row(conv_ln_g[0]), row(conv_ln_b[0])
    wcp = w_conv_proj[0].astype(BF16)
    bcp = row(b_conv_proj[0])
    wout = w_out[0].astype(BF16)
    gpre, gpost = row(norm_pre_mix[0]), row(norm_post_mix[0])
    gffn, gffn_post = row(norm_pre_ffn[0]), row(norm_post_ffn[0])

    nch = 2 * N_FFN_CHUNKS
    wup = w_up[0].astype(BF16)
    dwp = jnp.concatenate(
        [ffn_dw_w[0], ffn_dw_b[0][None, :], jnp.zeros((SUBLANES - FFN_CONV_K - 1, 2 * FFN_DIM), F32)],
        axis=0)
    dwp = dwp * jnp.concatenate([jnp.full((FFN_DIM,), 0.5, F32), jnp.ones((FFN_DIM,), F32)])
    wdown = w_down[0].astype(BF16)

    rc, rs1, rs2, cosT, sinT = _rope_tables(N_META + seq)

    smem = pl.BlockSpec(memory_space=pltpu.SMEM)
    vmem = pl.BlockSpec(memory_space=pltpu.VMEM)

    k_meta, v_meta, glu_meta, u_meta = pl.pallas_call(
        _meta_kernel,
        out_shape=(jax.ShapeDtypeStruct((N_META, KV_WIDTH), F32),
                   jax.ShapeDtypeStruct((N_META, KV_WIDTH), F32),
                   jax.ShapeDtypeStruct((N_META, CONV_CH), F32),
                   jax.ShapeDtypeStruct((N_META, 2 * FFN_DIM), F32)),
        in_specs=[vmem] * 12 + [smem] + [vmem] * 14,
        out_specs=(vmem, vmem, vmem, vmem),
        scratch_shapes=[pltpu.VMEM((CONV_HALO + N_META, CONV_CH), F32)],
        compiler_params=pltpu.CompilerParams(vmem_limit_bytes=VMEM_LIMIT),
        name="meta_prologue",
    )(meta_tokens.astype(F32), gpre, wqT, row(bq), wk, bk, wvT, row(bv), wglu, bglu, wgate,
      bgate, sinks, wap, dww, dwb, lng, lnb, wcp, bcp, wout, gpost, gffn, wup,
      rc[:N_META], rs1[:N_META], rs2[:N_META])

    vmetaT = v_meta.T.astype(BF16)
    umeta_tail = u_meta[N_META - SUBLANES:, :]

    T = MIX_TILE
    bqT = jnp.broadcast_to(bq[:, None], (ATTN_WIDTH, T)).astype(F32)
    bvT = jnp.broadcast_to(bv[:, None], (KV_WIDTH, T)).astype(F32)
    tile_spec = pl.BlockSpec((1, T, D_MODEL), lambda b, t: (b, t, 0))
    rope_spec = pl.BlockSpec((T, LANES), lambda b, t: (t, 0))
    ropeT_spec = pl.BlockSpec((SUBLANES, T), lambda b, t: (0, t))
    mixer_in = [
        (x, tile_spec), (gpre, None), (wqT, None), (bqT, None), (wk, None), (bk, None),
        (wvT, None), (bvT, None), (wglu, None), (bglu, None), (wgate, None), (bgate, None),
        (sinks, smem), (wap, None), (dww8, None), (dwb, None), (lng, None), (lnb, None),
        (wcp, None), (bcp, None), (wout, None), (gpost, None),
        (rc[N_META:], rope_spec), (rs1[N_META:], rope_spec), (rs2[N_META:], rope_spec),
        (cosT[:, N_META:], ropeT_spec), (sinT[:, N_META:], ropeT_spec),
        (k_meta, None), (vmetaT, None), (glu_meta, None),
    ]
    h2 = pl.pallas_call(
        _mixer_kernel,
        out_shape=jax.ShapeDtypeStruct((bsz, seq, D_MODEL), F32),
        grid=(bsz, seq // T),
        in_specs=[spec if spec is not None else _const_spec(a.shape) for a, spec in mixer_in],
        out_specs=tile_spec,
        scratch_shapes=[
            pltpu.VMEM((4, BLOCK + T, LANES), BF16),
            pltpu.VMEM((KV_WIDTH, BLOCK + T), BF16),
            pltpu.VMEM((CONV_HALO + T, CONV_CH), F32),
            pltpu.VMEM((CONV_SHIFTS - 1, T + CONV_PAD, CONV_CH), F32),
            pltpu.VMEM((ATTN_WIDTH, T), BF16),
            pltpu.VMEM((T // BLOCK, ATTN_WIDTH, BLOCK), F32),
            pltpu.VMEM((T, D_MODEL), F32),
            pltpu.VMEM((T, CONV_CH), F32),
            pltpu.VMEM((2, 2 * BLOCK, 2 * BLOCK), F32),
        ],
        compiler_params=pltpu.CompilerParams(
            dimension_semantics=("arbitrary", "arbitrary"), vmem_limit_bytes=VMEM_LIMIT),
        name="mixer",
    )(*[a for a, _ in mixer_in])

    T2 = FFN_TILE
    tile2 = pl.BlockSpec((1, T2, D_MODEL), lambda b, t: (b, t, 0))
    ffn_in = [(h2, tile2), (gffn, None), (gffn_post, None), (wup, None), (dwp, None),
              (wdown, None), (umeta_tail, None)]
    out = pl.pallas_call(
        _ffn_kernel,
        out_shape=jax.ShapeDtypeStruct((bsz, seq, D_MODEL), F32),
        grid=(bsz, seq // T2),
        in_specs=[spec if spec is not None else _const_spec(a.shape) for a, spec in ffn_in],
        out_specs=tile2,
        scratch_shapes=[
            pltpu.VMEM((T2, D_MODEL), BF16),
            pltpu.VMEM((FFN_UP_AHEAD + 1, 2, SUBLANES + T2, FFN_CHUNK), F32),
            pltpu.VMEM((SUBLANES, 2 * FFN_DIM), F32),
            pltpu.VMEM((T2, FFN_DIM), BF16),
            pltpu.VMEM((T2, D_MODEL), F32),
        ],
        compiler_params=pltpu.CompilerParams(
            dimension_semantics=("arbitrary", "arbitrary"), vmem_limit_bytes=VMEM_LIMIT),
        name="conv_ffn",
    )(*[a for a, _ in ffn_in])
    return out.astype(x.dtype)
```

```python
import functools

import numpy as np
import jax
import jax.numpy as jnp
from jax import lax
from jax.experimental import pallas as pl
from jax.experimental.pallas import tpu as pltpu

D_MODEL = 1024
N_META = 16
N_Q_HEADS = 16
N_KV_HEADS = 2
HEAD_DIM = 64
GROUP = N_Q_HEADS // N_KV_HEADS
ROT_DIM = HEAD_DIM // 4
ROT_HALF = ROT_DIM // 2
ROPE_THETA = 500000.0
WINDOW = 128
BLOCK = 128
ATTN_WIDTH = N_Q_HEADS * HEAD_DIM
KV_WIDTH = N_KV_HEADS * HEAD_DIM
CONV_CH = D_MODEL
CONV_K = 31
FFN_DIM = 2816
FFN_CONV_K = 3
RMS_EPS = 1e-6
LN_EPS = 1e-5
NEG_INF = -1e30

LANES = 128
SUBLANES = 8
MIX_TILE = 512
FFN_TILE = 512
FFN_CHUNK = 256
N_FFN_CHUNKS = FFN_DIM // FFN_CHUNK
FFN_DOWN_GROUPS = ((0, 3), (3, 6), (6, 9), (9, 11))
FFN_UP_AHEAD = 2
CONV_HALO = 32
CONV_SHIFTS = SUBLANES
CONV_PAD = CONV_HALO - SUBLANES
CONV_ROWS = 32
CONV_COLS = 256
ATTN_LOOKAHEAD = 1
GATE_CHUNKS = 4
PV_DELAY = 0
ATTN_PROJ_DELAY = 0
VMEM_LIMIT = 56 * 1024 * 1024

F32 = jnp.float32
BF16 = jnp.bfloat16

_NT = (((1,), (1,)), ((), ()))


def _dot(a, b):
    return jnp.dot(a, b, preferred_element_type=F32)


def _dot_nt(a, b):
    return lax.dot_general(a, b, _NT, preferred_element_type=F32)


def _rms(x, g):
    ms = jnp.mean(x * x, axis=-1, keepdims=True)
    return x * lax.rsqrt(ms + RMS_EPS) * g


def _layer_norm(x, g, b):
    mu = jnp.mean(x, axis=-1, keepdims=True)
    xc = x - mu
    var = jnp.mean(xc * xc, axis=-1, keepdims=True)
    return xc * lax.rsqrt(var + LN_EPS) * g + b


def _sigmoid(x):
    return 0.5 * jnp.tanh(0.5 * x) + 0.5


def _silu_2x(hx):
    return hx * jnp.tanh(hx) + hx


def _silu(x):
    return _silu_2x(0.5 * x)


def _rope_rows(z, c, s1, s2):
    parts = []
    for g in range(z.shape[1] // LANES):
        zg = z[:, g * LANES:(g + 1) * LANES]
        parts.append(zg * c + pltpu.roll(zg, LANES - ROT_HALF, 1) * s1
                     + pltpu.roll(zg, ROT_HALF, 1) * s2)
    return parts[0] if len(parts) == 1 else jnp.concatenate(parts, axis=1)


def _key_variants(k):
    lane = lax.broadcasted_iota(jnp.int32, k.shape, 1)
    lo = lane < HEAD_DIM
    kr = pltpu.roll(k, HEAD_DIM, 1)
    zero = jnp.zeros_like(k)
    return [jnp.where(lo, k, zero).astype(BF16), jnp.where(lo, zero, kr).astype(BF16),
            jnp.where(lo, kr, zero).astype(BF16), jnp.where(lo, zero, k).astype(BF16)]


def _meta_kernel(meta_ref, gpre_ref, wqT_ref, bq_ref, wk_ref, bk_ref, wvT_ref, bv_ref,
                 wglu_ref, bglu_ref, wgate_ref, bgate_ref, sink_ref, wap_ref, dww_ref,
                 dwb_ref, lng_ref, lnb_ref, wcp_ref, bcp_ref, wout_ref, gpost_ref,
                 gffn_ref, wup_ref, rc_ref, rs1_ref, rs2_ref,
                 k_out, v_out, glu_out, u_out, gscr):
    m = meta_ref[...]
    hn = _rms(m, gpre_ref[...]).astype(BF16)
    q = _dot_nt(hn, wqT_ref[...]) + bq_ref[...]
    k = _dot(hn, wk_ref[...]) + bk_ref[...]
    v = _dot_nt(hn, wvT_ref[...]) + bv_ref[...]
    rc, rs1, rs2 = rc_ref[...], rs1_ref[...], rs2_ref[...]
    q = _rope_rows(q, rc, rs1, rs2)
    k = _rope_rows(k, rc, rs1, rs2)
    k_out[...] = k
    v_out[...] = v

    row = lax.broadcasted_iota(jnp.int32, (N_META, N_META), 0)
    col = lax.broadcasted_iota(jnp.int32, (N_META, N_META), 1)
    causal = col <= row
    qb, kb, vb = q.astype(BF16), k.astype(BF16), v.astype(BF16)
    attn = jnp.zeros((N_META, D_MODEL), F32)
    for hq in range(N_Q_HEADS):
        h = hq // GROUP
        qh = qb[:, hq * HEAD_DIM:(hq + 1) * HEAD_DIM]
        kh = kb[:, h * HEAD_DIM:(h + 1) * HEAD_DIM]
        vh = vb[:, h * HEAD_DIM:(h + 1) * HEAD_DIM]
        s = jnp.where(causal, _dot_nt(qh, kh), NEG_INF)
        sink = sink_ref[hq]
        mx = jnp.maximum(jnp.max(s, axis=-1, keepdims=True), sink)
        p = jnp.exp(s - mx)
        den = jnp.sum(p, axis=-1, keepdims=True) + jnp.exp(sink - mx)
        o = _dot(p.astype(BF16), vh) / den
        attn = attn + _dot(o.astype(BF16), wap_ref[hq * HEAD_DIM:(hq + 1) * HEAD_DIM, :])

    glu_in = _dot(hn, wglu_ref[...]) + bglu_ref[...]
    glu = glu_in[:, :CONV_CH] * _sigmoid(glu_in[:, CONV_CH:])
    glu_out[...] = glu
    gscr[0:CONV_HALO, :] = jnp.zeros((CONV_HALO, CONV_CH), F32)
    gscr[CONV_HALO:CONV_HALO + N_META, :] = glu
    c = jnp.broadcast_to(dwb_ref[...], (N_META, CONV_CH))
    off = CONV_HALO - (CONV_K - 1)
    for kk in range(CONV_K):
        c = c + gscr[off + kk:off + kk + N_META, :] * dww_ref[kk:kk + 1, :]
    c = _silu(_layer_norm(c, lng_ref[...], lnb_ref[...]))
    conv = _dot(c.astype(BF16), wcp_ref[...]) + bcp_ref[...]

    gates = _sigmoid(_dot(hn, wgate_ref[...]) + bgate_ref[...])
    merged = gates[:, :D_MODEL] * attn + gates[:, D_MODEL:] * conv
    mix = _dot(merged.astype(BF16), wout_ref[...])
    h2 = m + _rms(mix, gpost_ref[...])
    hn2 = _rms(h2, gffn_ref[...]).astype(BF16)
    u_out[...] = _dot(hn2, wup_ref[...])


def _mixer_kernel(x_ref, gpre_ref, wqT_ref, bqT_ref, wk_ref, bk_ref, wvT_ref, bvT_ref,
                  wglu_ref, bglu_ref, wgate_ref, bgate_ref, sink_ref, wap_ref, dww_ref,
                  dwb_ref, lng_ref, lnb_ref, wcp_ref, bcp_ref, wout_ref, gpost_ref,
                  rc_ref, rs1_ref, rs2_ref, cosT_ref, sinT_ref,
                  kmeta_ref, vmetaT_ref, glumeta_ref,
                  out_ref,
                  kbuf, vTbuf, gbuf, shbuf, qT_s, oT_s, attn_s, cbuf, bias_s):
    T = MIX_TILE
    nblk = T // BLOCK
    t = pl.program_id(1)

    @pl.when(t == 0)
    def _init():
        kbuf[:, 0:BLOCK, :] = jnp.zeros((4, BLOCK, LANES), BF16)
        vTbuf[:, 0:BLOCK] = jnp.zeros((KV_WIDTH, BLOCK), BF16)
        gbuf[0:CONV_HALO - N_META, :] = jnp.zeros((CONV_HALO - N_META, CONV_CH), F32)
        gbuf[CONV_HALO - N_META:CONV_HALO, :] = glumeta_ref[...]
        key = lax.broadcasted_iota(jnp.int32, (2 * BLOCK, 2 * BLOCK), 0)
        qry = lax.broadcasted_iota(jnp.int32, (2 * BLOCK, 2 * BLOCK), 1) % BLOCK
        vis = (key > qry) & (key <= qry + WINDOW)
        bias_s[0] = jnp.where(vis, 0.0, NEG_INF).astype(F32)
        bias_s[1] = jnp.where(vis & (key >= BLOCK), 0.0, NEG_INF).astype(F32)

    x = x_ref[0]
    hn = _rms(x, gpre_ref[...]).astype(BF16)

    glu_in = _dot(hn, wglu_ref[...]) + bglu_ref[...]
    gbuf[CONV_HALO:CONV_HALO + T, :] = glu_in[:, :CONV_CH] * _sigmoid(glu_in[:, CONV_CH:])
    for s in range(1, CONV_SHIFTS):
        shbuf[s - 1] = gbuf[SUBLANES - s:SUBLANES - s + T + CONV_PAD, :]

    qT = _dot_nt(wqT_ref[...], hn) + bqT_ref[...]
    cosT, sinT = cosT_ref[...], sinT_ref[...]
    for hq in range(N_Q_HEADS):
        base = hq * HEAD_DIM
        r0 = qT[base:base + ROT_HALF]
        r1 = qT[base + ROT_HALF:base + ROT_DIM]
        rot = jnp.concatenate([r0 * cosT - r1 * sinT, r1 * cosT + r0 * sinT], axis=0)
        qT_s[base:base + ROT_DIM, :] = rot.astype(BF16)
        qT_s[base + ROT_DIM:base + HEAD_DIM, :] = qT[base + ROT_DIM:base + HEAD_DIM].astype(BF16)

    k = _dot(hn, wk_ref[...]) + bk_ref[...]
    k = _rope_rows(k, rc_ref[...], rs1_ref[...], rs2_ref[...])
    for i, kv in enumerate(_key_variants(k)):
        kbuf[i, BLOCK:BLOCK + T, :] = kv
    kmeta_var = _key_variants(kmeta_ref[...])
    vT = _dot_nt(wvT_ref[...], hn) + bvT_ref[...]
    vTbuf[:, BLOCK:BLOCK + T] = vT.astype(BF16)
    vmetaT = vmetaT_ref[...]

    lane2 = lax.broadcasted_iota(jnp.int32, (1, 2 * BLOCK), 1)
    first_sel = jnp.where(t == 0, 1, 0)

    items = [(j, gp, hl) for j in range(nblk) for gp in range(4) for hl in range(2)]

    def scores(item):
        j, gp, hl = item
        h = gp // 2
        c0 = j * BLOCK
        rhs = jnp.concatenate(
            [qT_s[256 * gp:256 * gp + 128, c0:c0 + BLOCK],
             qT_s[256 * gp + 128:256 * gp + 256, c0:c0 + BLOCK]], axis=1)
        s_loc = _dot(kbuf[2 * h + hl, c0:c0 + 2 * BLOCK, :], rhs)
        s_met = _dot(kmeta_var[2 * h + hl], rhs)
        return s_loc, s_met

    def softmax(item, s_loc, s_met):
        j, gp, hl = item
        ha = 4 * gp + hl
        hb = ha + 2
        s_loc = s_loc + (bias_s[first_sel] if j == 0 else bias_s[0])
        sink = jnp.where(lane2 < BLOCK, sink_ref[ha], sink_ref[hb])
        mx = jnp.maximum(jnp.maximum(jnp.max(s_loc, axis=0, keepdims=True),
                                     jnp.max(s_met, axis=0, keepdims=True)), sink)
        p_loc = jnp.exp(s_loc - mx)
        p_met = jnp.exp(s_met - mx)
        den = (jnp.sum(p_loc, axis=0, keepdims=True)
               + jnp.sum(p_met, axis=0, keepdims=True) + jnp.exp(sink - mx))
        return p_loc.astype(BF16), p_met.astype(BF16), 1.0 / den

    def weighted_values(item, p_loc, p_met, inv_den):
        j, gp, hl = item
        h = gp // 2
        c0 = j * BLOCK
        ha = 4 * gp + hl
        hb = ha + 2
        vT_h = vTbuf[h * HEAD_DIM:(h + 1) * HEAD_DIM, c0:c0 + 2 * BLOCK]
        vmT_h = vmetaT[h * HEAD_DIM:(h + 1) * HEAD_DIM, :]
        oT = (_dot(vT_h, p_loc) + _dot(vmT_h, p_met)) * inv_den
        oT_s[j, ha * HEAD_DIM:(ha + 1) * HEAD_DIM, :] = oT[:, :BLOCK]
        oT_s[j, hb * HEAD_DIM:(hb + 1) * HEAD_DIM, :] = oT[:, BLOCK:]

    def conv_block(cc, rc):
        cols = slice(cc * CONV_COLS, (cc + 1) * CONV_COLS)
        r0 = rc * CONV_ROWS
        nsub = CONV_ROWS // SUBLANES
        bias8 = jnp.broadcast_to(dwb_ref[:, cols], (SUBLANES, CONV_COLS))
        accs = [bias8] * nsub
        for d in range(CONV_K):
            a, s = divmod(d, SUBLANES)
            w = dww_ref[CONV_K - 1 - d, :, cols]
            for r in range(nsub):
                if s == 0:
                    lo = r0 + r * SUBLANES + CONV_HALO - SUBLANES * a
                    g = gbuf[lo:lo + SUBLANES, cols]
                else:
                    lo = r0 + r * SUBLANES + CONV_PAD - SUBLANES * a
                    g = shbuf[s - 1, lo:lo + SUBLANES, cols]
                accs[r] = accs[r] + g * w
        for r in range(nsub):
            cbuf[r0 + r * SUBLANES:r0 + (r + 1) * SUBLANES, cols] = accs[r]

    conv_blocks = [(cc, rc) for cc in range(CONV_CH // CONV_COLS) for rc in range(T // CONV_ROWS)]
    conv_per_item = -(-len(conv_blocks) // len(items))

    n_items = len(items)
    items_per_blk = n_items // nblk
    items_per_gate = n_items // GATE_CHUNKS
    gate_w = 2 * D_MODEL // GATE_CHUNKS
    pending = [scores(it) for it in items[:ATTN_LOOKAHEAD]]
    probs = []
    gate_parts = []
    for step in range(n_items + PV_DELAY + ATTN_PROJ_DELAY):
        if step + ATTN_LOOKAHEAD < n_items:
            pending.append(scores(items[step + ATTN_LOOKAHEAD]))
        if step < n_items:
            probs.append(softmax(items[step], *pending.pop(0)))
        i = step - PV_DELAY
        if 0 <= i < n_items:
            weighted_values(items[i], *probs[i])
            probs[i] = None
        for _ in range(conv_per_item):
            if conv_blocks:
                conv_block(*conv_blocks.pop(0))
        if step < n_items and (step + 1) % items_per_gate == 0:
            gcols = slice(step // items_per_gate * gate_w, (step // items_per_gate + 1) * gate_w)
            gate_parts.append(_sigmoid(_dot(hn, wgate_ref[:, gcols]) + bgate_ref[:, gcols]))
        i = step - PV_DELAY - ATTN_PROJ_DELAY
        if 0 <= i < n_items and (i + 1) % items_per_blk == 0:
            j = i // items_per_blk
            o_blk = oT_s[j].T.astype(BF16)
            attn_s[j * BLOCK:(j + 1) * BLOCK, :] = _dot(o_blk, wap_ref[...])
    while conv_blocks:
        conv_block(*conv_blocks.pop(0))
    gates = jnp.concatenate(gate_parts, axis=1)

    kbuf[:, 0:BLOCK, :] = kbuf[:, T:T + BLOCK, :]
    vTbuf[:, 0:BLOCK] = vTbuf[:, T:T + BLOCK]
    gbuf[0:CONV_HALO, :] = gbuf[T:T + CONV_HALO, :]

    c = _silu(_layer_norm(cbuf[...], lng_ref[...], lnb_ref[...]))
    conv = _dot(c.astype(BF16), wcp_ref[...]) + bcp_ref[...]

    merged = gates[:, :D_MODEL] * attn_s[...] + gates[:, D_MODEL:] * conv
    mix = _dot(merged.astype(BF16), wout_ref[...])
    out_ref[0] = x + _rms(mix, gpost_ref[...])


def _ffn_cols(cidx, half):
    start = half * FFN_DIM + cidx * FFN_CHUNK
    return slice(start, start + FFN_CHUNK)


def _ffn_kernel(h_ref, gpre_ref, gpost_ref, wup_ref, dwp_ref, wdown_ref, umeta_ref,
                out_ref, hn_s, ubuf, carry, act_s, acc_s):
    T = FFN_TILE
    t = pl.program_id(1)

    @pl.when(t == 0)
    def _init():
        carry[...] = umeta_ref[...]

    h = h_ref[0]
    hn_s[...] = _rms(h, gpre_ref[...]).astype(BF16)

    def up(cidx):
        par = cidx % (FFN_UP_AHEAD + 1)
        for half in range(2):
            cols = _ffn_cols(cidx, half)
            u = _dot(hn_s[...], wup_ref[:, cols])
            ubuf[par, half, 0:SUBLANES, :] = carry[:, cols]
            ubuf[par, half, SUBLANES:SUBLANES + T, :] = u
            carry[:, cols] = u[T - SUBLANES:T, :]

    def activate(cidx):
        par = cidx % (FFN_UP_AHEAD + 1)
        ys = []
        for half in range(2):
            w = dwp_ref[:, _ffn_cols(cidx, half)]
            ue = ubuf[par, half]
            delayed = [pltpu.roll(ue, d, 0)[SUBLANES:SUBLANES + T, :] if d else ue[SUBLANES:SUBLANES + T, :]
                       for d in range(FFN_CONV_K)]
            ys.append(w[3:4, :] + w[0:1, :] * delayed[2] + w[1:2, :] * delayed[1]
                      + w[2:3, :] * delayed[0])
        act_s[:, cidx * FFN_CHUNK:(cidx + 1) * FFN_CHUNK] = (_silu_2x(ys[0]) * ys[1]).astype(BF16)

    total = None
    for cidx in range(FFN_UP_AHEAD):
        up(cidx)
    for cidx in range(N_FFN_CHUNKS):
        if cidx + FFN_UP_AHEAD < N_FFN_CHUNKS:
            up(cidx + FFN_UP_AHEAD)
        activate(cidx)
        for lo, hi in FFN_DOWN_GROUPS:
            if cidx == hi - 1:
                part = _dot(act_s[:, lo * FFN_CHUNK:hi * FFN_CHUNK],
                            wdown_ref[lo * FFN_CHUNK:hi * FFN_CHUNK, :])
                if hi == N_FFN_CHUNKS:
                    total = acc_s[...] + part
                elif lo == 0:
                    acc_s[...] = part
                else:
                    acc_s[...] += part
    out_ref[0] = h + _rms(total, gpost_ref[...])


def _const_spec(shape):
    nd = len(shape)
    return pl.BlockSpec(shape, lambda *_: (0,) * nd, pipeline_mode=pl.Buffered(1))


def _rope_tables(n_pos):
    inv_freq = ROPE_THETA ** (-jnp.arange(ROT_HALF, dtype=F32) * 2.0 / ROT_DIM)
    ang = jnp.arange(n_pos).astype(F32)[:, None] * inv_freq[None, :]
    cos, sin = jnp.cos(ang), jnp.sin(ang)
    ones = jnp.ones((n_pos, HEAD_DIM - ROT_DIM), F32)
    zeros = jnp.zeros((n_pos, HEAD_DIM - ROT_HALF), F32)
    c = jnp.concatenate([cos, cos, ones], axis=1)
    s1 = jnp.concatenate([-sin, zeros], axis=1)
    s2 = jnp.concatenate([zeros[:, :ROT_HALF], sin, zeros[:, :HEAD_DIM - ROT_DIM]], axis=1)
    rep = LANES // HEAD_DIM
    return (jnp.tile(c, (1, rep)), jnp.tile(s1, (1, rep)), jnp.tile(s2, (1, rep)),
            cos.T, sin.T)


def kernel(x, meta_tokens, norm_pre_mix, norm_post_mix, w_in, b_in, attn_sinks, w_attn_proj,
           conv_dw_w, conv_dw_b, conv_ln_g, conv_ln_b, w_conv_proj, b_conv_proj, w_out,
           norm_pre_ffn, norm_post_ffn, w_up, ffn_dw_w, ffn_dw_b, w_down):
    bsz, seq, _ = x.shape
    assert seq % MIX_TILE == 0 and seq % FFN_TILE == 0
    scale = HEAD_DIM ** -0.5
    row = lambda v: v.reshape(1, -1).astype(F32)

    wi, bi = w_in[0], b_in[0]
    c_q, c_k, c_v, c_glu = ATTN_WIDTH, ATTN_WIDTH + KV_WIDTH, ATTN_WIDTH + 2 * KV_WIDTH, \
        ATTN_WIDTH + 2 * KV_WIDTH + 2 * CONV_CH
    wqT = (wi[:, :c_q] * scale).T.astype(BF16)
    bq = bi[:c_q] * scale
    wk = wi[:, c_q:c_k].astype(BF16)
    bk = row(bi[c_q:c_k])
    wvT = wi[:, c_k:c_v].T.astype(BF16)
    bv = bi[c_k:c_v]
    wglu = wi[:, c_v:c_glu].astype(BF16)
    bglu = row(bi[c_v:c_glu])
    wgate = wi[:, c_glu:].astype(BF16)
    bgate = row(bi[c_glu:])
    sinks = attn_sinks[0].astype(F32)
    wap = w_attn_proj[0].astype(BF16)
    dww = conv_dw_w[0].astype(F32)
    dww8 = jnp.broadcast_to(dww[:, None, :], (CONV_K, SUBLANES, CONV_CH))
    dwb = row(conv_dw_b[0])
    lng, lnb = row(conv_ln_g[0]), row(conv_ln_b[0])
    wcp = w_conv_proj[0].astype(BF16)
    bcp = row(b_conv_proj[0])
    wout = w_out[0].astype(BF16)
    gpre, gpost = row(norm_pre_mix[0]), row(norm_post_mix[0])
    gffn, gffn_post = row(norm_pre_ffn[0]), row(norm_post_ffn[0])

    nch = 2 * N_FFN_CHUNKS
    wup = w_up[0].astype(BF16)
    dwp = jnp.concatenate(
        [ffn_dw_w[0], ffn_dw_b[0][None, :], jnp.zeros((SUBLANES - FFN_CONV_K - 1, 2 * FFN_DIM), F32)],
        axis=0)
    dwp = dwp * jnp.concatenate([jnp.full((FFN_DIM,), 0.5, F32), jnp.ones((FFN_DIM,), F32)])
    wdown = w_down[0].astype(BF16)

    rc, rs1, rs2, cosT, sinT = _rope_tables(N_META + seq)

    smem = pl.BlockSpec(memory_space=pltpu.SMEM)
    vmem = pl.BlockSpec(memory_space=pltpu.VMEM)

    k_meta, v_meta, glu_meta, u_meta = pl.pallas_call(
        _meta_kernel,
        out_shape=(jax.ShapeDtypeStruct((N_META, KV_WIDTH), F32),
                   jax.ShapeDtypeStruct((N_META, KV_WIDTH), F32),
                   jax.ShapeDtypeStruct((N_META, CONV_CH), F32),
                   jax.ShapeDtypeStruct((N_META, 2 * FFN_DIM), F32)),
        in_specs=[vmem] * 12 + [smem] + [vmem] * 14,
        out_specs=(vmem, vmem, vmem, vmem),
        scratch_shapes=[pltpu.VMEM((CONV_HALO + N_META, CONV_CH), F32)],
        compiler_params=pltpu.CompilerParams(vmem_limit_bytes=VMEM_LIMIT),
        name="meta_prologue",
    )(meta_tokens.astype(F32), gpre, wqT, row(bq), wk, bk, wvT, row(bv), wglu, bglu, wgate,
      bgate, sinks, wap, dww, dwb, lng, lnb, wcp, bcp, wout, gpost, gffn, wup,
      rc[:N_META], rs1[:N_META], rs2[:N_META])

    vmetaT = v_meta.T.astype(BF16)
    umeta_tail = u_meta[N_META - SUBLANES:, :]

    T = MIX_TILE
    bqT = jnp.broadcast_to(bq[:, None], (ATTN_WIDTH, T)).astype(F32)
    bvT = jnp.broadcast_to(bv[:, None], (KV_WIDTH, T)).astype(F32)
    tile_spec = pl.BlockSpec((1, T, D_MODEL), lambda b, t: (b, t, 0))
    rope_spec = pl.BlockSpec((T, LANES), lambda b, t: (t, 0))
    ropeT_spec = pl.BlockSpec((SUBLANES, T), lambda b, t: (0, t))
    mixer_in = [
        (x, tile_spec), (gpre, None), (wqT, None), (bqT, None), (wk, None), (bk, None),
        (wvT, None), (bvT, None), (wglu, None), (bglu, None), (wgate, None), (bgate, None),
        (sinks, smem), (wap, None), (dww8, None), (dwb, None), (lng, None), (lnb, None),
        (wcp, None), (bcp, None), (wout, None), (gpost, None),
        (rc[N_META:], rope_spec), (rs1[N_META:], rope_spec), (rs2[N_META:], rope_spec),
        (cosT[:, N_META:], ropeT_spec), (sinT[:, N_META:], ropeT_spec),
        (k_meta, None), (vmetaT, None), (glu_meta, None),
    ]
    h2 = pl.pallas_call(
        _mixer_kernel,
        out_shape=jax.ShapeDtypeStruct((bsz, seq, D_MODEL), F32),
        grid=(bsz, seq // T),
        in_specs=[spec if spec is not None else _const_spec(a.shape) for a, spec in mixer_in],
        out_specs=tile_spec,
        scratch_shapes=[
            pltpu.VMEM((4, BLOCK + T, LANES), BF16),
            pltpu.VMEM((KV_WIDTH, BLOCK + T), BF16),
            pltpu.VMEM((CONV_HALO + T, CONV_CH), F32),
            pltpu.VMEM((CONV_SHIFTS - 1, T + CONV_PAD, CONV_CH), F32),
            pltpu.VMEM((ATTN_WIDTH, T), BF16),
            pltpu.VMEM((T // BLOCK, ATTN_WIDTH, BLOCK), F32),
            pltpu.VMEM((T, D_MODEL), F32),
            pltpu.VMEM((T, CONV_CH), F32),
            pltpu.VMEM((2, 2 * BLOCK, 2 * BLOCK), F32),
        ],
        compiler_params=pltpu.CompilerParams(
            dimension_semantics=("arbitrary", "arbitrary"), vmem_limit_bytes=VMEM_LIMIT),
        name="mixer",
    )(*[a for a, _ in mixer_in])

    T2 = FFN_TILE
    tile2 = pl.BlockSpec((1, T2, D_MODEL), lambda b, t: (b, t, 0))
    ffn_in = [(h2, tile2), (gffn, None), (gffn_post, None), (wup, None), (dwp, None),
              (wdown, None), (umeta_tail, None)]
    out = pl.pallas_call(
        _ffn_kernel,
        out_shape=jax.ShapeDtypeStruct((bsz, seq, D_MODEL), F32),
        grid=(bsz, seq // T2),
        in_specs=[spec if spec is not None else _const_spec(a.shape) for a, spec in ffn_in],
        out_specs=tile2,
        scratch_shapes=[
            pltpu.VMEM((T2, D_MODEL), BF16),
            pltpu.VMEM((FFN_UP_AHEAD + 1, 2, SUBLANES + T2, FFN_CHUNK), F32),
            pltpu.VMEM((SUBLANES, 2 * FFN_DIM), F32),
            pltpu.VMEM((T2, FFN_DIM), BF16),
            pltpu.VMEM((T2, D_MODEL), F32),
        ],
        compiler_params=pltpu.CompilerParams(
            dimension_semantics=("arbitrary", "arbitrary"), vmem_limit_bytes=VMEM_LIMIT),
        name="conv_ffn",
    )(*[a for a, _ in ffn_in])
    return out.astype(x.dtype)
```

```python
import functools

import numpy as np
import jax
import jax.numpy as jnp
from jax import lax
from jax.experimental import pallas as pl
from jax.experimental.pallas import tpu as pltpu

D_MODEL = 1024
N_META = 16
N_Q_HEADS = 16
N_KV_HEADS = 2
HEAD_DIM = 64
GROUP = N_Q_HEADS // N_KV_HEADS
ROT_DIM = HEAD_DIM // 4
ROT_HALF = ROT_DIM // 2
ROPE_THETA = 500000.0
WINDOW = 128
BLOCK = 128
ATTN_WIDTH = N_Q_HEADS * HEAD_DIM
KV_WIDTH = N_KV_HEADS * HEAD_DIM
CONV_CH = D_MODEL
CONV_K = 31
FFN_DIM = 2816
FFN_CONV_K = 3
RMS_EPS = 1e-6
LN_EPS = 1e-5
NEG_INF = -1e30

LANES = 128
SUBLANES = 8
MIX_TILE = 512
FFN_TILE = 512
FFN_CHUNK = 256
N_FFN_CHUNKS = FFN_DIM // FFN_CHUNK
FFN_DOWN_GROUPS = ((0, 3), (3, 6), (6, 9), (9, 11))
FFN_UP_AHEAD = 2
CONV_HALO = 32
CONV_SHIFTS = SUBLANES
CONV_PAD = CONV_HALO - SUBLANES
CONV_ROWS = 32
CONV_COLS = 256
ATTN_LOOKAHEAD = 3
GATE_CHUNKS = 4
PV_DELAY = 0
ATTN_PROJ_DELAY = 0
VMEM_LIMIT = 56 * 1024 * 1024

F32 = jnp.float32
BF16 = jnp.bfloat16

_NT = (((1,), (1,)), ((), ()))


def _dot(a, b):
    return jnp.dot(a, b, preferred_element_type=F32)


def _dot_nt(a, b):
    return lax.dot_general(a, b, _NT, preferred_element_type=F32)


def _rms(x, g):
    ms = jnp.mean(x * x, axis=-1, keepdims=True)
    return x * lax.rsqrt(ms + RMS_EPS) * g


def _layer_norm(x, g, b):
    mu = jnp.mean(x, axis=-1, keepdims=True)
    xc = x - mu
    var = jnp.mean(xc * xc, axis=-1, keepdims=True)
    return xc * lax.rsqrt(var + LN_EPS) * g + b


def _sigmoid(x):
    return 0.5 * jnp.tanh(0.5 * x) + 0.5


def _silu_2x(hx):
    return hx * jnp.tanh(hx) + hx


def _silu(x):
    return _silu_2x(0.5 * x)


def _rope_rows(z, c, s1, s2):
    parts = []
    for g in range(z.shape[1] // LANES):
        zg = z[:, g * LANES:(g + 1) * LANES]
        parts.append(zg * c + pltpu.roll(zg, LANES - ROT_HALF, 1) * s1
                     + pltpu.roll(zg, ROT_HALF, 1) * s2)
    return parts[0] if len(parts) == 1 else jnp.concatenate(parts, axis=1)


def _key_variants(k):
    lane = lax.broadcasted_iota(jnp.int32, k.shape, 1)
    lo = lane < HEAD_DIM
    kr = pltpu.roll(k, HEAD_DIM, 1)
    zero = jnp.zeros_like(k)
    return [jnp.where(lo, k, zero).astype(BF16), jnp.where(lo, zero, kr).astype(BF16),
            jnp.where(lo, kr, zero).astype(BF16), jnp.where(lo, zero, k).astype(BF16)]


def _meta_kernel(meta_ref, gpre_ref, wqT_ref, bq_ref, wk_ref, bk_ref, wvT_ref, bv_ref,
                 wglu_ref, bglu_ref, wgate_ref, bgate_ref, sink_ref, wap_ref, dww_ref,
                 dwb_ref, lng_ref, lnb_ref, wcp_ref, bcp_ref, wout_ref, gpost_ref,
                 gffn_ref, wup_ref, rc_ref, rs1_ref, rs2_ref,
                 k_out, v_out, glu_out, u_out, gscr):
    m = meta_ref[...]
    hn = _rms(m, gpre_ref[...]).astype(BF16)
    q = _dot_nt(hn, wqT_ref[...]) + bq_ref[...]
    k = _dot(hn, wk_ref[...]) + bk_ref[...]
    v = _dot_nt(hn, wvT_ref[...]) + bv_ref[...]
    rc, rs1, rs2 = rc_ref[...], rs1_ref[...], rs2_ref[...]
    q = _rope_rows(q, rc, rs1, rs2)
    k = _rope_rows(k, rc, rs1, rs2)
    k_out[...] = k
    v_out[...] = v

    row = lax.broadcasted_iota(jnp.int32, (N_META, N_META), 0)
    col = lax.broadcasted_iota(jnp.int32, (N_META, N_META), 1)
    causal = col <= row
    qb, kb, vb = q.astype(BF16), k.astype(BF16), v.astype(BF16)
    attn = jnp.zeros((N_META, D_MODEL), F32)
    for hq in range(N_Q_HEADS):
        h = hq // GROUP
        qh = qb[:, hq * HEAD_DIM:(hq + 1) * HEAD_DIM]
        kh = kb[:, h * HEAD_DIM:(h + 1) * HEAD_DIM]
        vh = vb[:, h * HEAD_DIM:(h + 1) * HEAD_DIM]
        s = jnp.where(causal, _dot_nt(qh, kh), NEG_INF)
        sink = sink_ref[hq]
        mx = jnp.maximum(jnp.max(s, axis=-1, keepdims=True), sink)
        p = jnp.exp(s - mx)
        den = jnp.sum(p, axis=-1, keepdims=True) + jnp.exp(sink - mx)
        o = _dot(p.astype(BF16), vh) / den
        attn = attn + _dot(o.astype(BF16), wap_ref[hq * HEAD_DIM:(hq + 1) * HEAD_DIM, :])

    glu_in = _dot(hn, wglu_ref[...]) + bglu_ref[...]
    glu = glu_in[:, :CONV_CH] * _sigmoid(glu_in[:, CONV_CH:])
    glu_out[...] = glu
    gscr[0:CONV_HALO, :] = jnp.zeros((CONV_HALO, CONV_CH), F32)
    gscr[CONV_HALO:CONV_HALO + N_META, :] = glu
    c = jnp.broadcast_to(dwb_ref[...], (N_META, CONV_CH))
    off = CONV_HALO - (CONV_K - 1)
    for kk in range(CONV_K):
        c = c + gscr[off + kk:off + kk + N_META, :] * dww_ref[kk:kk + 1, :]
    c = _silu(_layer_norm(c, lng_ref[...], lnb_ref[...]))
    conv = _dot(c.astype(BF16), wcp_ref[...]) + bcp_ref[...]

    gates = _sigmoid(_dot(hn, wgate_ref[...]) + bgate_ref[...])
    merged = gates[:, :D_MODEL] * attn + gates[:, D_MODEL:] * conv
    mix = _dot(merged.astype(BF16), wout_ref[...])
    h2 = m + _rms(mix, gpost_ref[...])
    hn2 = _rms(h2, gffn_ref[...]).astype(BF16)
    u_out[...] = _dot(hn2, wup_ref[...])


def _mixer_kernel(x_ref, gpre_ref, wqT_ref, bqT_ref, wk_ref, bk_ref, wvT_ref, bvT_ref,
                  wglu_ref, bglu_ref, wgate_ref, bgate_ref, sink_ref, wap_ref, dww_ref,
                  dwb_ref, lng_ref, lnb_ref, wcp_ref, bcp_ref, wout_ref, gpost_ref,
                  rc_ref, rs1_ref, rs2_ref, cosT_ref, sinT_ref,
                  kmeta_ref, vmetaT_ref, glumeta_ref,
                  out_ref,
                  kbuf, vTbuf, gbuf, shbuf, qT_s, oT_s, attn_s, cbuf, bias_s):
    T = MIX_TILE
    nblk = T // BLOCK
    t = pl.program_id(1)

    @pl.when(t == 0)
    def _init():
        kbuf[:, 0:BLOCK, :] = jnp.zeros((4, BLOCK, LANES), BF16)
        vTbuf[:, 0:BLOCK] = jnp.zeros((KV_WIDTH, BLOCK), BF16)
        gbuf[0:CONV_HALO - N_META, :] = jnp.zeros((CONV_HALO - N_META, CONV_CH), F32)
        gbuf[CONV_HALO - N_META:CONV_HALO, :] = glumeta_ref[...]
        key = lax.broadcasted_iota(jnp.int32, (2 * BLOCK, 2 * BLOCK), 0)
        qry = lax.broadcasted_iota(jnp.int32, (2 * BLOCK, 2 * BLOCK), 1) % BLOCK
        vis = (key > qry) & (key <= qry + WINDOW)
        bias_s[0] = jnp.where(vis, 0.0, NEG_INF).astype(F32)
        bias_s[1] = jnp.where(vis & (key >= BLOCK), 0.0, NEG_INF).astype(F32)

    x = x_ref[0]
    hn = _rms(x, gpre_ref[...]).astype(BF16)

    glu_in = _dot(hn, wglu_ref[...]) + bglu_ref[...]
    gbuf[CONV_HALO:CONV_HALO + T, :] = glu_in[:, :CONV_CH] * _sigmoid(glu_in[:, CONV_CH:])
    for s in range(1, CONV_SHIFTS):
        shbuf[s - 1] = gbuf[SUBLANES - s:SUBLANES - s + T + CONV_PAD, :]

    qT = _dot_nt(wqT_ref[...], hn) + bqT_ref[...]
    cosT, sinT = cosT_ref[...], sinT_ref[...]
    for hq in range(N_Q_HEADS):
        base = hq * HEAD_DIM
        r0 = qT[base:base + ROT_HALF]
        r1 = qT[base + ROT_HALF:base + ROT_DIM]
        rot = jnp.concatenate([r0 * cosT - r1 * sinT, r1 * cosT + r0 * sinT], axis=0)
        qT_s[base:base + ROT_DIM, :] = rot.astype(BF16)
        qT_s[base + ROT_DIM:base + HEAD_DIM, :] = qT[base + ROT_DIM:base + HEAD_DIM].astype(BF16)

    k = _dot(hn, wk_ref[...]) + bk_ref[...]
    k = _rope_rows(k, rc_ref[...], rs1_ref[...], rs2_ref[...])
    for i, kv in enumerate(_key_variants(k)):
        kbuf[i, BLOCK:BLOCK + T, :] = kv
    kmeta_var = _key_variants(kmeta_ref[...])
    vT = _dot_nt(wvT_ref[...], hn) + bvT_ref[...]
    vTbuf[:, BLOCK:BLOCK + T] = vT.astype(BF16)
    vmetaT = vmetaT_ref[...]

    lane2 = lax.broadcasted_iota(jnp.int32, (1, 2 * BLOCK), 1)
    first_sel = jnp.where(t == 0, 1, 0)

    items = [(j, gp, hl) for j in range(nblk) for gp in range(4) for hl in range(2)]

    def scores(item):
        j, gp, hl = item
        h = gp // 2
        c0 = j * BLOCK
        rhs = jnp.concatenate(
            [qT_s[256 * gp:256 * gp + 128, c0:c0 + BLOCK],
             qT_s[256 * gp + 128:256 * gp + 256, c0:c0 + BLOCK]], axis=1)
        s_loc = _dot(kbuf[2 * h + hl, c0:c0 + 2 * BLOCK, :], rhs)
        s_met = _dot(kmeta_var[2 * h + hl], rhs)
        return s_loc, s_met

    def softmax(item, s_loc, s_met):
        j, gp, hl = item
        ha = 4 * gp + hl
        hb = ha + 2
        s_loc = s_loc + (bias_s[first_sel] if j == 0 else bias_s[0])
        sink = jnp.where(lane2 < BLOCK, sink_ref[ha], sink_ref[hb])
        mx = jnp.maximum(jnp.maximum(jnp.max(s_loc, axis=0, keepdims=True),
                                     jnp.max(s_met, axis=0, keepdims=True)), sink)
        p_loc = jnp.exp(s_loc - mx)
        p_met = jnp.exp(s_met - mx)
        den = (jnp.sum(p_loc, axis=0, keepdims=True)
               + jnp.sum(p_met, axis=0, keepdims=True) + jnp.exp(sink - mx))
        return p_loc.astype(BF16), p_met.astype(BF16), 1.0 / den

    def weighted_values(item, p_loc, p_met, inv_den):
        j, gp, hl = item
        h = gp // 2
        c0 = j * BLOCK
        ha = 4 * gp + hl
        hb = ha + 2
        vT_h = vTbuf[h * HEAD_DIM:(h + 1) * HEAD_DIM, c0:c0 + 2 * BLOCK]
        vmT_h = vmetaT[h * HEAD_DIM:(h + 1) * HEAD_DIM, :]
        oT = (_dot(vT_h, p_loc) + _dot(vmT_h, p_met)) * inv_den
        oT_s[j, ha * HEAD_DIM:(ha + 1) * HEAD_DIM, :] = oT[:, :BLOCK]
        oT_s[j, hb * HEAD_DIM:(hb + 1) * HEAD_DIM, :] = oT[:, BLOCK:]

    def conv_block(cc, rc):
        cols = slice(cc * CONV_COLS, (cc + 1) * CONV_COLS)
        r0 = rc * CONV_ROWS
        nsub = CONV_ROWS // SUBLANES
        bias8 = jnp.broadcast_to(dwb_ref[:, cols], (SUBLANES, CONV_COLS))
        accs = [bias8] * nsub
        for d in range(CONV_K):
            a, s = divmod(d, SUBLANES)
            w = dww_ref[CONV_K - 1 - d, :, cols]
            for r in range(nsub):
                if s == 0:
                    lo = r0 + r * SUBLANES + CONV_HALO - SUBLANES * a
                    g = gbuf[lo:lo + SUBLANES, cols]
                else:
                    lo = r0 + r * SUBLANES + CONV_PAD - SUBLANES * a
                    g = shbuf[s - 1, lo:lo + SUBLANES, cols]
                accs[r] = accs[r] + g * w
        for r in range(nsub):
            cbuf[r0 + r * SUBLANES:r0 + (r + 1) * SUBLANES, cols] = accs[r]

    conv_blocks = [(cc, rc) for cc in range(CONV_CH // CONV_COLS) for rc in range(T // CONV_ROWS)]
    conv_per_item = -(-len(conv_blocks) // len(items))

    n_items = len(items)
    items_per_blk = n_items // nblk
    items_per_gate = n_items // GATE_CHUNKS
    gate_w = 2 * D_MODEL // GATE_CHUNKS
    pending = [scores(it) for it in items[:ATTN_LOOKAHEAD]]
    probs = []
    gate_parts = []
    for step in range(n_items + PV_DELAY + ATTN_PROJ_DELAY):
        if step + ATTN_LOOKAHEAD < n_items:
            pending.append(scores(items[step + ATTN_LOOKAHEAD]))
        if step < n_items:
            probs.append(softmax(items[step], *pending.pop(0)))
        i = step - PV_DELAY
        if 0 <= i < n_items:
            weighted_values(items[i], *probs[i])
            probs[i] = None
        for _ in range(conv_per_item):
            if conv_blocks:
                conv_block(*conv_blocks.pop(0))
        if step < n_items and (step + 1) % items_per_gate == 0:
            gcols = slice(step // items_per_gate * gate_w, (step // items_per_gate + 1) * gate_w)
            gate_parts.append(_sigmoid(_dot(hn, wgate_ref[:, gcols]) + bgate_ref[:, gcols]))
        i = step - PV_DELAY - ATTN_PROJ_DELAY
        if 0 <= i < n_items and (i + 1) % items_per_blk == 0:
            j = i // items_per_blk
            o_blk = oT_s[j].T.astype(BF16)
            attn_s[j * BLOCK:(j + 1) * BLOCK, :] = _dot(o_blk, wap_ref[...])
    while conv_blocks:
        conv_block(*conv_blocks.pop(0))
    gates = jnp.concatenate(gate_parts, axis=1)

    kbuf[:, 0:BLOCK, :] = kbuf[:, T:T + BLOCK, :]
    vTbuf[:, 0:BLOCK] = vTbuf[:, T:T + BLOCK]
    gbuf[0:CONV_HALO, :] = gbuf[T:T + CONV_HALO, :]

    c = _silu(_layer_norm(cbuf[...], lng_ref[...], lnb_ref[...]))
    conv = _dot(c.astype(BF16), wcp_ref[...]) + bcp_ref[...]

    merged = gates[:, :D_MODEL] * attn_s[...] + gates[:, D_MODEL:] * conv
    mix = _dot(merged.astype(BF16), wout_ref[...])
    out_ref[0] = x + _rms(mix, gpost_ref[...])


def _ffn_cols(cidx, half):
    start = half * FFN_DIM + cidx * FFN_CHUNK
    return slice(start, start + FFN_CHUNK)


def _ffn_kernel(h_ref, gpre_ref, gpost_ref, wup_ref, dwp_ref, wdown_ref, umeta_ref,
                out_ref, hn_s, ubuf, carry, act_s, acc_s):
    T = FFN_TILE
    t = pl.program_id(1)

    @pl.when(t == 0)
    def _init():
        carry[...] = umeta_ref[...]

    h = h_ref[0]
    hn_s[...] = _rms(h, gpre_ref[...]).astype(BF16)

    def up(cidx):
        par = cidx % (FFN_UP_AHEAD + 1)
        for half in range(2):
            cols = _ffn_cols(cidx, half)
            u = _dot(hn_s[...], wup_ref[:, cols])
            ubuf[par, half, 0:SUBLANES, :] = carry[:, cols]
            ubuf[par, half, SUBLANES:SUBLANES + T, :] = u
            carry[:, cols] = u[T - SUBLANES:T, :]

    def activate(cidx):
        par = cidx % (FFN_UP_AHEAD + 1)
        ys = []
        for half in range(2):
            w = dwp_ref[:, _ffn_cols(cidx, half)]
            ue = ubuf[par, half]
            delayed = [pltpu.roll(ue, d, 0)[SUBLANES:SUBLANES + T, :] if d else ue[SUBLANES:SUBLANES + T, :]
                       for d in range(FFN_CONV_K)]
            ys.append(w[3:4, :] + w[0:1, :] * delayed[2] + w[1:2, :] * delayed[1]
                      + w[2:3, :] * delayed[0])
        act_s[:, cidx * FFN_CHUNK:(cidx + 1) * FFN_CHUNK] = (_silu_2x(ys[0]) * ys[1]).astype(BF16)

    total = None
    for cidx in range(FFN_UP_AHEAD):
        up(cidx)
    for cidx in range(N_FFN_CHUNKS):
        if cidx + FFN_UP_AHEAD < N_FFN_CHUNKS:
            up(cidx + FFN_UP_AHEAD)
        activate(cidx)
        for lo, hi in FFN_DOWN_GROUPS:
            if cidx == hi - 1:
                part = _dot(act_s[:, lo * FFN_CHUNK:hi * FFN_CHUNK],
                            wdown_ref[lo * FFN_CHUNK:hi * FFN_CHUNK, :])
                if hi == N_FFN_CHUNKS:
                    total = acc_s[...] + part
                elif lo == 0:
                    acc_s[...] = part
                else:
                    acc_s[...] += part
    out_ref[0] = h + _rms(total, gpost_ref[...])


def _const_spec(shape):
    nd = len(shape)
    return pl.BlockSpec(shape, lambda *_: (0,) * nd, pipeline_mode=pl.Buffered(1))


def _rope_tables(n_pos):
    inv_freq = ROPE_THETA ** (-jnp.arange(ROT_HALF, dtype=F32) * 2.0 / ROT_DIM)
    ang = jnp.arange(n_pos).astype(F32)[:, None] * inv_freq[None, :]
    cos, sin = jnp.cos(ang), jnp.sin(ang)
    ones = jnp.ones((n_pos, HEAD_DIM - ROT_DIM), F32)
    zeros = jnp.zeros((n_pos, HEAD_DIM - ROT_HALF), F32)
    c = jnp.concatenate([cos, cos, ones], axis=1)
    s1 = jnp.concatenate([-sin, zeros], axis=1)
    s2 = jnp.concatenate([zeros[:, :ROT_HALF], sin, zeros[:, :HEAD_DIM - ROT_DIM]], axis=1)
    rep = LANES // HEAD_DIM
    return (jnp.tile(c, (1, rep)), jnp.tile(s1, (1, rep)), jnp.tile(s2, (1, rep)),
            cos.T, sin.T)


def kernel(x, meta_tokens, norm_pre_mix, norm_post_mix, w_in, b_in, attn_sinks, w_attn_proj,
           conv_dw_w, conv_dw_b, conv_ln_g, conv_ln_b, w_conv_proj, b_conv_proj, w_out,
           norm_pre_ffn, norm_post_ffn, w_up, ffn_dw_w, ffn_dw_b, w_down):
    bsz, seq, _ = x.shape
    assert seq % MIX_TILE == 0 and seq % FFN_TILE == 0
    scale = HEAD_DIM ** -0.5
    row = lambda v: v.reshape(1, -1).astype(F32)

    wi, bi = w_in[0], b_in[0]
    c_q, c_k, c_v, c_glu = ATTN_WIDTH, ATTN_WIDTH + KV_WIDTH, ATTN_WIDTH + 2 * KV_WIDTH, \
        ATTN_WIDTH + 2 * KV_WIDTH + 2 * CONV_CH
    wqT = (wi[:, :c_q] * scale).T.astype(BF16)
    bq = bi[:c_q] * scale
    wk = wi[:, c_q:c_k].astype(BF16)
    bk = row(bi[c_q:c_k])
    wvT = wi[:, c_k:c_v].T.astype(BF16)
    bv = bi[c_k:c_v]
    wglu = wi[:, c_v:c_glu].astype(BF16)
    bglu = row(bi[c_v:c_glu])
    wgate = wi[:, c_glu:].astype(BF16)
    bgate = row(bi[c_glu:])
    sinks = attn_sinks[0].astype(F32)
    wap = w_attn_proj[0].astype(BF16)
    dww = conv_dw_w[0].astype(F32)
    dww8 = jnp.broadcast_to(dww[:, None, :], (CONV_K, SUBLANES, CONV_CH))
    dwb = row(conv_dw_b[0])
    lng, lnb = row(conv_ln_g[0]), row(conv_ln_b[0])
    wcp = w_conv_proj[0].astype(BF16)
    bcp = row(b_conv_proj[0])
    wout = w_out[0].astype(BF16)
    gpre, gpost = row(norm_pre_mix[0]), row(norm_post_mix[0])
    gffn, gffn_post = row(norm_pre_ffn[0]), row(norm_post_ffn[0])

    nch = 2 * N_FFN_CHUNKS
    wup = w_up[0].astype(BF16)
    dwp = jnp.concatenate(
        [ffn_dw_w[0], ffn_dw_b[0][None, :], jnp.zeros((SUBLANES - FFN_CONV_K - 1, 2 * FFN_DIM), F32)],
        axis=0)
    dwp = dwp * jnp.concatenate([jnp.full((FFN_DIM,), 0.5, F32), jnp.ones((FFN_DIM,), F32)])
    wdown = w_down[0].astype(BF16)

    rc, rs1, rs2, cosT, sinT = _rope_tables(N_META + seq)

    smem = pl.BlockSpec(memory_space=pltpu.SMEM)
    vmem = pl.BlockSpec(memory_space=pltpu.VMEM)

    k_meta, v_meta, glu_meta, u_meta = pl.pallas_call(
        _meta_kernel,
        out_shape=(jax.ShapeDtypeStruct((N_META, KV_WIDTH), F32),
                   jax.ShapeDtypeStruct((N_META, KV_WIDTH), F32),
                   jax.ShapeDtypeStruct((N_META, CONV_CH), F32),
                   jax.ShapeDtypeStruct((N_META, 2 * FFN_DIM), F32)),
        in_specs=[vmem] * 12 + [smem] + [vmem] * 14,
        out_specs=(vmem, vmem, vmem, vmem),
        scratch_shapes=[pltpu.VMEM((CONV_HALO + N_META, CONV_CH), F32)],
        compiler_params=pltpu.CompilerParams(vmem_limit_bytes=VMEM_LIMIT),
        name="meta_prologue",
    )(meta_tokens.astype(F32), gpre, wqT, row(bq), wk, bk, wvT, row(bv), wglu, bglu, wgate,
      bgate, sinks, wap, dww, dwb, lng, lnb, wcp, bcp, wout, gpost, gffn, wup,
      rc[:N_META], rs1[:N_META], rs2[:N_META])

    vmetaT = v_meta.T.astype(BF16)
    umeta_tail = u_meta[N_META - SUBLANES:, :]

    T = MIX_TILE
    bqT = jnp.broadcast_to(bq[:, None], (ATTN_WIDTH, T)).astype(F32)
    bvT = jnp.broadcast_to(bv[:, None], (KV_WIDTH, T)).astype(F32)
    tile_spec = pl.BlockSpec((1, T, D_MODEL), lambda b, t: (b, t, 0))
    rope_spec = pl.BlockSpec((T, LANES), lambda b, t: (t, 0))
    ropeT_spec = pl.BlockSpec((SUBLANES, T), lambda b, t: (0, t))
    mixer_in = [
        (x, tile_spec), (gpre, None), (wqT, None), (bqT, None), (wk, None), (bk, None),
        (wvT, None), (bvT, None), (wglu, None), (bglu, None), (wgate, None), (bgate, None),
        (sinks, smem), (wap, None), (dww8, None), (dwb, None), (lng, None), (lnb, None),
        (wcp, None), (bcp, None), (wout, None), (gpost, None),
        (rc[N_META:], rope_spec), (rs1[N_META:], rope_spec), (rs2[N_META:], rope_spec),
        (cosT[:, N_META:], ropeT_spec), (sinT[:, N_META:], ropeT_spec),
        (k_meta, None), (vmetaT, None), (glu_meta, None),
    ]
    h2 = pl.pallas_call(
        _mixer_kernel,
        out_shape=jax.ShapeDtypeStruct((bsz, seq, D_MODEL), F32),
        grid=(bsz, seq // T),
        in_specs=[spec if spec is not None else _const_spec(a.shape) for a, spec in mixer_in],
        out_specs=tile_spec,
        scratch_shapes=[
            pltpu.VMEM((4, BLOCK + T, LANES), BF16),
            pltpu.VMEM((KV_WIDTH, BLOCK + T), BF16),
            pltpu.VMEM((CONV_HALO + T, CONV_CH), F32),
            pltpu.VMEM((CONV_SHIFTS - 1, T + CONV_PAD, CONV_CH), F32),
            pltpu.VMEM((ATTN_WIDTH, T), BF16),
            pltpu.VMEM((T // BLOCK, ATTN_WIDTH, BLOCK), F32),
            pltpu.VMEM((T, D_MODEL), F32),
            pltpu.VMEM((T, CONV_CH), F32),
            pltpu.VMEM((2, 2 * BLOCK, 2 * BLOCK), F32),
        ],
        compiler_params=pltpu.CompilerParams(
            dimension_semantics=("arbitrary", "arbitrary"), vmem_limit_bytes=VMEM_LIMIT),
        name="mixer",
    )(*[a for a, _ in mixer_in])

    T2 = FFN_TILE
    tile2 = pl.BlockSpec((1, T2, D_MODEL), lambda b, t: (b, t, 0))
    ffn_in = [(h2, tile2), (gffn, None), (gffn_post, None), (wup, None), (dwp, None),
              (wdown, None), (umeta_tail, None)]
    out = pl.pallas_call(
        _ffn_kernel,
        out_shape=jax.ShapeDtypeStruct((bsz, seq, D_MODEL), F32),
        grid=(bsz, seq // T2),
        in_specs=[spec if spec is not None else _const_spec(a.shape) for a, spec in ffn_in],
        out_specs=tile2,
        scratch_shapes=[
            pltpu.VMEM((T2, D_MODEL), BF16),
            pltpu.VMEM((FFN_UP_AHEAD + 1, 2, SUBLANES + T2, FFN_CHUNK), F32),
            pltpu.VMEM((SUBLANES, 2 * FFN_DIM), F32),
            pltpu.VMEM((T2, FFN_DIM), BF16),
            pltpu.VMEM((T2, D_MODEL), F32),
        ],
        compiler_params=pltpu.CompilerParams(
            dimension_semantics=("arbitrary", "arbitrary"), vmem_limit_bytes=VMEM_LIMIT),
        name="conv_ffn",
    )(*[a for a, _ in ffn_in])
    return out.astype(x.dtype)
```

```python
import functools

import numpy as np
import jax
import jax.numpy as jnp
from jax import lax
from jax.experimental import pallas as pl
from jax.experimental.pallas import tpu as pltpu

D_MODEL = 1024
N_META = 16
N_Q_HEADS = 16
N_KV_HEADS = 2
HEAD_DIM = 64
GROUP = N_Q_HEADS // N_KV_HEADS
ROT_DIM = HEAD_DIM // 4
ROT_HALF = ROT_DIM // 2
ROPE_THETA = 500000.0
WINDOW = 128
BLOCK = 128
ATTN_WIDTH = N_Q_HEADS * HEAD_DIM
KV_WIDTH = N_KV_HEADS * HEAD_DIM
CONV_CH = D_MODEL
CONV_K = 31
FFN_DIM = 2816
FFN_CONV_K = 3
RMS_EPS = 1e-6
LN_EPS = 1e-5
NEG_INF = -1e30

LANES = 128
SUBLANES = 8
MIX_TILE = 512
FFN_TILE = 512
FFN_CHUNK = 256
N_FFN_CHUNKS = FFN_DIM // FFN_CHUNK
FFN_DOWN_GROUPS = ((0, 3), (3, 6), (6, 9), (9, 11))
FFN_UP_AHEAD = 3
CONV_HALO = 32
CONV_SHIFTS = SUBLANES
CONV_PAD = CONV_HALO - SUBLANES
CONV_ROWS = 32
CONV_COLS = 256
ATTN_LOOKAHEAD = 2
GATE_CHUNKS = 8
PV_DELAY = 0
ATTN_PROJ_DELAY = 0
VMEM_LIMIT = 56 * 1024 * 1024

F32 = jnp.float32
BF16 = jnp.bfloat16

_NT = (((1,), (1,)), ((), ()))


def _dot(a, b):
    return jnp.dot(a, b, preferred_element_type=F32)


def _dot_nt(a, b):
    return lax.dot_general(a, b, _NT, preferred_element_type=F32)


def _rms(x, g):
    ms = jnp.mean(x * x, axis=-1, keepdims=True)
    return x * lax.rsqrt(ms + RMS_EPS) * g


def _layer_norm(x, g, b):
    mu = jnp.mean(x, axis=-1, keepdims=True)
    xc = x - mu
    var = jnp.mean(xc * xc, axis=-1, keepdims=True)
    return xc * lax.rsqrt(var + LN_EPS) * g + b


def _sigmoid(x):
    return 0.5 * jnp.tanh(0.5 * x) + 0.5


def _silu_2x(hx):
    return hx * jnp.tanh(hx) + hx


def _silu(x):
    return _silu_2x(0.5 * x)


def _rope_rows(z, c, s1, s2):
    parts = []
    for g in range(z.shape[1] // LANES):
        zg = z[:, g * LANES:(g + 1) * LANES]
        parts.append(zg * c + pltpu.roll(zg, LANES - ROT_HALF, 1) * s1
                     + pltpu.roll(zg, ROT_HALF, 1) * s2)
    return parts[0] if len(parts) == 1 else jnp.concatenate(parts, axis=1)


def _key_variants(k):
    lane = lax.broadcasted_iota(jnp.int32, k.shape, 1)
    lo = lane < HEAD_DIM
    kr = pltpu.roll(k, HEAD_DIM, 1)
    zero = jnp.zeros_like(k)
    return [jnp.where(lo, k, zero).astype(BF16), jnp.where(lo, zero, kr).astype(BF16),
            jnp.where(lo, kr, zero).astype(BF16), jnp.where(lo, zero, k).astype(BF16)]


def _meta_kernel(meta_ref, gpre_ref, wqT_ref, bq_ref, wk_ref, bk_ref, wvT_ref, bv_ref,
                 wglu_ref, bglu_ref, wgate_ref, bgate_ref, sink_ref, wap_ref, dww_ref,
                 dwb_ref, lng_ref, lnb_ref, wcp_ref, bcp_ref, wout_ref, gpost_ref,
                 gffn_ref, wup_ref, rc_ref, rs1_ref, rs2_ref,
                 k_out, v_out, glu_out, u_out, gscr):
    m = meta_ref[...]
    hn = _rms(m, gpre_ref[...]).astype(BF16)
    q = _dot_nt(hn, wqT_ref[...]) + bq_ref[...]
    k = _dot(hn, wk_ref[...]) + bk_ref[...]
    v = _dot_nt(hn, wvT_ref[...]) + bv_ref[...]
    rc, rs1, rs2 = rc_ref[...], rs1_ref[...], rs2_ref[...]
    q = _rope_rows(q, rc, rs1, rs2)
    k = _rope_rows(k, rc, rs1, rs2)
    k_out[...] = k
    v_out[...] = v

    row = lax.broadcasted_iota(jnp.int32, (N_META, N_META), 0)
    col = lax.broadcasted_iota(jnp.int32, (N_META, N_META), 1)
    causal = col <= row
    qb, kb, vb = q.astype(BF16), k.astype(BF16), v.astype(BF16)
    attn = jnp.zeros((N_META, D_MODEL), F32)
    for hq in range(N_Q_HEADS):
        h = hq // GROUP
        qh = qb[:, hq * HEAD_DIM:(hq + 1) * HEAD_DIM]
        kh = kb[:, h * HEAD_DIM:(h + 1) * HEAD_DIM]
        vh = vb[:, h * HEAD_DIM:(h + 1) * HEAD_DIM]
        s = jnp.where(causal, _dot_nt(qh, kh), NEG_INF)
        sink = sink_ref[hq]
        mx = jnp.maximum(jnp.max(s, axis=-1, keepdims=True), sink)
        p = jnp.exp(s - mx)
        den = jnp.sum(p, axis=-1, keepdims=True) + jnp.exp(sink - mx)
        o = _dot(p.astype(BF16), vh) / den
        attn = attn + _dot(o.astype(BF16), wap_ref[hq * HEAD_DIM:(hq + 1) * HEAD_DIM, :])

    glu_in = _dot(hn, wglu_ref[...]) + bglu_ref[...]
    glu = glu_in[:, :CONV_CH] * _sigmoid(glu_in[:, CONV_CH:])
    glu_out[...] = glu
    gscr[0:CONV_HALO, :] = jnp.zeros((CONV_HALO, CONV_CH), F32)
    gscr[CONV_HALO:CONV_HALO + N_META, :] = glu
    c = jnp.broadcast_to(dwb_ref[...], (N_META, CONV_CH))
    off = CONV_HALO - (CONV_K - 1)
    for kk in range(CONV_K):
        c = c + gscr[off + kk:off + kk + N_META, :] * dww_ref[kk:kk + 1, :]
    c = _silu(_layer_norm(c, lng_ref[...], lnb_ref[...]))
    conv = _dot(c.astype(BF16), wcp_ref[...]) + bcp_ref[...]

    gates = _sigmoid(_dot(hn, wgate_ref[...]) + bgate_ref[...])
    merged = gates[:, :D_MODEL] * attn + gates[:, D_MODEL:] * conv
    mix = _dot(merged.astype(BF16), wout_ref[...])
    h2 = m + _rms(mix, gpost_ref[...])
    hn2 = _rms(h2, gffn_ref[...]).astype(BF16)
    u_out[...] = _dot(hn2, wup_ref[...])


def _mixer_kernel(x_ref, gpre_ref, wqT_ref, bqT_ref, wk_ref, bk_ref, wvT_ref, bvT_ref,
                  wglu_ref, bglu_ref, wgate_ref, bgate_ref, sink_ref, wap_ref, dww_ref,
                  dwb_ref, lng_ref, lnb_ref, wcp_ref, bcp_ref, wout_ref, gpost_ref,
                  rc_ref, rs1_ref, rs2_ref, cosT_ref, sinT_ref,
                  kmeta_ref, vmetaT_ref, glumeta_ref,
                  out_ref,
                  kbuf, vTbuf, gbuf, shbuf, qT_s, oT_s, attn_s, cbuf, bias_s):
    T = MIX_TILE
    nblk = T // BLOCK
    t = pl.program_id(1)

    @pl.when(t == 0)
    def _init():
        kbuf[:, 0:BLOCK, :] = jnp.zeros((4, BLOCK, LANES), BF16)
        vTbuf[:, 0:BLOCK] = jnp.zeros((KV_WIDTH, BLOCK), BF16)
        gbuf[0:CONV_HALO - N_META, :] = jnp.zeros((CONV_HALO - N_META, CONV_CH), F32)
        gbuf[CONV_HALO - N_META:CONV_HALO, :] = glumeta_ref[...]
        key = lax.broadcasted_iota(jnp.int32, (2 * BLOCK, 2 * BLOCK), 0)
        qry = lax.broadcasted_iota(jnp.int32, (2 * BLOCK, 2 * BLOCK), 1) % BLOCK
        vis = (key > qry) & (key <= qry + WINDOW)
        bias_s[0] = jnp.where(vis, 0.0, NEG_INF).astype(F32)
        bias_s[1] = jnp.where(vis & (key >= BLOCK), 0.0, NEG_INF).astype(F32)

    x = x_ref[0]
    hn = _rms(x, gpre_ref[...]).astype(BF16)

    glu_in = _dot(hn, wglu_ref[...]) + bglu_ref[...]
    gbuf[CONV_HALO:CONV_HALO + T, :] = glu_in[:, :CONV_CH] * _sigmoid(glu_in[:, CONV_CH:])
    for s in range(1, CONV_SHIFTS):
        shbuf[s - 1] = gbuf[SUBLANES - s:SUBLANES - s + T + CONV_PAD, :]

    qT = _dot_nt(wqT_ref[...], hn) + bqT_ref[...]
    cosT, sinT = cosT_ref[...], sinT_ref[...]
    for hq in range(N_Q_HEADS):
        base = hq * HEAD_DIM
        r0 = qT[base:base + ROT_HALF]
        r1 = qT[base + ROT_HALF:base + ROT_DIM]
        rot = jnp.concatenate([r0 * cosT - r1 * sinT, r1 * cosT + r0 * sinT], axis=0)
        qT_s[base:base + ROT_DIM, :] = rot.astype(BF16)
        qT_s[base + ROT_DIM:base + HEAD_DIM, :] = qT[base + ROT_DIM:base + HEAD_DIM].astype(BF16)

    k = _dot(hn, wk_ref[...]) + bk_ref[...]
    k = _rope_rows(k, rc_ref[...], rs1_ref[...], rs2_ref[...])
    for i, kv in enumerate(_key_variants(k)):
        kbuf[i, BLOCK:BLOCK + T, :] = kv
    kmeta_var = _key_variants(kmeta_ref[...])
    vT = _dot_nt(wvT_ref[...], hn) + bvT_ref[...]
    vTbuf[:, BLOCK:BLOCK + T] = vT.astype(BF16)
    vmetaT = vmetaT_ref[...]

    lane2 = lax.broadcasted_iota(jnp.int32, (1, 2 * BLOCK), 1)
    first_sel = jnp.where(t == 0, 1, 0)

    items = [(j, gp, hl) for j in range(nblk) for gp in range(4) for hl in range(2)]

    def scores(item):
        j, gp, hl = item
        h = gp // 2
        c0 = j * BLOCK
        rhs = jnp.concatenate(
            [qT_s[256 * gp:256 * gp + 128, c0:c0 + BLOCK],
             qT_s[256 * gp + 128:256 * gp + 256, c0:c0 + BLOCK]], axis=1)
        s_loc = _dot(kbuf[2 * h + hl, c0:c0 + 2 * BLOCK, :], rhs)
        s_met = _dot(kmeta_var[2 * h + hl], rhs)
        return s_loc, s_met

    def softmax(item, s_loc, s_met):
        j, gp, hl = item
        ha = 4 * gp + hl
        hb = ha + 2
        s_loc = s_loc + (bias_s[first_sel] if j == 0 else bias_s[0])
        sink = jnp.where(lane2 < BLOCK, sink_ref[ha], sink_ref[hb])
        mx = jnp.maximum(jnp.maximum(jnp.max(s_loc, axis=0, keepdims=True),
                                     jnp.max(s_met, axis=0, keepdims=True)), sink)
        p_loc = jnp.exp(s_loc - mx)
        p_met = jnp.exp(s_met - mx)
        den = (jnp.sum(p_loc, axis=0, keepdims=True)
               + jnp.sum(p_met, axis=0, keepdims=True) + jnp.exp(sink - mx))
        return p_loc.astype(BF16), p_met.astype(BF16), 1.0 / den

    def weighted_values(item, p_loc, p_met, inv_den):
        j, gp, hl = item
        h = gp // 2
        c0 = j * BLOCK
        ha = 4 * gp + hl
        hb = ha + 2
        vT_h = vTbuf[h * HEAD_DIM:(h + 1) * HEAD_DIM, c0:c0 + 2 * BLOCK]
        vmT_h = vmetaT[h * HEAD_DIM:(h + 1) * HEAD_DIM, :]
        oT = (_dot(vT_h, p_loc) + _dot(vmT_h, p_met)) * inv_den
        oT_s[j, ha * HEAD_DIM:(ha + 1) * HEAD_DIM, :] = oT[:, :BLOCK]
        oT_s[j, hb * HEAD_DIM:(hb + 1) * HEAD_DIM, :] = oT[:, BLOCK:]

    def conv_block(cc, rc):
        cols = slice(cc * CONV_COLS, (cc + 1) * CONV_COLS)
        r0 = rc * CONV_ROWS
        nsub = CONV_ROWS // SUBLANES
        bias8 = jnp.broadcast_to(dwb_ref[:, cols], (SUBLANES, CONV_COLS))
        accs = [bias8] * nsub
        for d in range(CONV_K):
            a, s = divmod(d, SUBLANES)
            w = dww_ref[CONV_K - 1 - d, :, cols]
            for r in range(nsub):
                if s == 0:
                    lo = r0 + r * SUBLANES + CONV_HALO - SUBLANES * a
                    g = gbuf[lo:lo + SUBLANES, cols]
                else:
                    lo = r0 + r * SUBLANES + CONV_PAD - SUBLANES * a
                    g = shbuf[s - 1, lo:lo + SUBLANES, cols]
                accs[r] = accs[r] + g * w
        for r in range(nsub):
            cbuf[r0 + r * SUBLANES:r0 + (r + 1) * SUBLANES, cols] = accs[r]

    conv_blocks = [(cc, rc) for cc in range(CONV_CH // CONV_COLS) for rc in range(T // CONV_ROWS)]
    conv_per_item = -(-len(conv_blocks) // len(items))

    n_items = len(items)
    items_per_blk = n_items // nblk
    items_per_gate = n_items // GATE_CHUNKS
    gate_w = 2 * D_MODEL // GATE_CHUNKS
    pending = [scores(it) for it in items[:ATTN_LOOKAHEAD]]
    probs = []
    gate_parts = []
    for step in range(n_items + PV_DELAY + ATTN_PROJ_DELAY):
        if step + ATTN_LOOKAHEAD < n_items:
            pending.append(scores(items[step + ATTN_LOOKAHEAD]))
        if step < n_items:
            probs.append(softmax(items[step], *pending.pop(0)))
        i = step - PV_DELAY
        if 0 <= i < n_items:
            weighted_values(items[i], *probs[i])
            probs[i] = None
        for _ in range(conv_per_item):
            if conv_blocks:
                conv_block(*conv_blocks.pop(0))
        if step < n_items and (step + 1) % items_per_gate == 0:
            gcols = slice(step // items_per_gate * gate_w, (step // items_per_gate + 1) * gate_w)
            gate_parts.append(_sigmoid(_dot(hn, wgate_ref[:, gcols]) + bgate_ref[:, gcols]))
        i = step - PV_DELAY - ATTN_PROJ_DELAY
        if 0 <= i < n_items and (i + 1) % items_per_blk == 0:
            j = i // items_per_blk
            o_blk = oT_s[j].T.astype(BF16)
            attn_s[j * BLOCK:(j + 1) * BLOCK, :] = _dot(o_blk, wap_ref[...])
    while conv_blocks:
        conv_block(*conv_blocks.pop(0))
    gates = jnp.concatenate(gate_parts, axis=1)

    kbuf[:, 0:BLOCK, :] = kbuf[:, T:T + BLOCK, :]
    vTbuf[:, 0:BLOCK] = vTbuf[:, T:T + BLOCK]
    gbuf[0:CONV_HALO, :] = gbuf[T:T + CONV_HALO, :]

    c = _silu(_layer_norm(cbuf[...], lng_ref[...], lnb_ref[...]))
    conv = _dot(c.astype(BF16), wcp_ref[...]) + bcp_ref[...]

    merged = gates[:, :D_MODEL] * attn_s[...] + gates[:, D_MODEL:] * conv
    mix = _dot(merged.astype(BF16), wout_ref[...])
    out_ref[0] = x + _rms(mix, gpost_ref[...])


def _ffn_cols(cidx, half):
    start = half * FFN_DIM + cidx * FFN_CHUNK
    return slice(start, start + FFN_CHUNK)


def _ffn_kernel(h_ref, gpre_ref, gpost_ref, wup_ref, dwp_ref, wdown_ref, umeta_ref,
                out_ref, hn_s, ubuf, carry, act_s, acc_s):
    T = FFN_TILE
    t = pl.program_id(1)

    @pl.when(t == 0)
    def _init():
        carry[...] = umeta_ref[...]

    h = h_ref[0]
    hn_s[...] = _rms(h, gpre_ref[...]).astype(BF16)

    def up(cidx):
        par = cidx % (FFN_UP_AHEAD + 1)
        for half in range(2):
            cols = _ffn_cols(cidx, half)
            u = _dot(hn_s[...], wup_ref[:, cols])
            ubuf[par, half, 0:SUBLANES, :] = carry[:, cols]
            ubuf[par, half, SUBLANES:SUBLANES + T, :] = u
            carry[:, cols] = u[T - SUBLANES:T, :]

    def activate(cidx):
        par = cidx % (FFN_UP_AHEAD + 1)
        ys = []
        for half in range(2):
            w = dwp_ref[:, _ffn_cols(cidx, half)]
            ue = ubuf[par, half]
            delayed = [pltpu.roll(ue, d, 0)[SUBLANES:SUBLANES + T, :] if d else ue[SUBLANES:SUBLANES + T, :]
                       for d in range(FFN_CONV_K)]
            ys.append(w[3:4, :] + w[0:1, :] * delayed[2] + w[1:2, :] * delayed[1]
                      + w[2:3, :] * delayed[0])
        act_s[:, cidx * FFN_CHUNK:(cidx + 1) * FFN_CHUNK] = (_silu_2x(ys[0]) * ys[1]).astype(BF16)

    total = None
    for cidx in range(FFN_UP_AHEAD):
        up(cidx)
    for cidx in range(N_FFN_CHUNKS):
        if cidx + FFN_UP_AHEAD < N_FFN_CHUNKS:
            up(cidx + FFN_UP_AHEAD)
        activate(cidx)
        for lo, hi in FFN_DOWN_GROUPS:
            if cidx == hi - 1:
                part = _dot(act_s[:, lo * FFN_CHUNK:hi * FFN_CHUNK],
                            wdown_ref[lo * FFN_CHUNK:hi * FFN_CHUNK, :])
                if hi == N_FFN_CHUNKS:
                    total = acc_s[...] + part
                elif lo == 0:
                    acc_s[...] = part
                else:
                    acc_s[...] += part
    out_ref[0] = h + _rms(total, gpost_ref[...])


def _const_spec(shape):
    nd = len(shape)
    return pl.BlockSpec(shape, lambda *_: (0,) * nd, pipeline_mode=pl.Buffered(1))


def _rope_tables(n_pos):
    inv_freq = ROPE_THETA ** (-jnp.arange(ROT_HALF, dtype=F32) * 2.0 / ROT_DIM)
    ang = jnp.arange(n_pos).astype(F32)[:, None] * inv_freq[None, :]
    cos, sin = jnp.cos(ang), jnp.sin(ang)
    ones = jnp.ones((n_pos, HEAD_DIM - ROT_DIM), F32)
    zeros = jnp.zeros((n_pos, HEAD_DIM - ROT_HALF), F32)
    c = jnp.concatenate([cos, cos, ones], axis=1)
    s1 = jnp.concatenate([-sin, zeros], axis=1)
    s2 = jnp.concatenate([zeros[:, :ROT_HALF], sin, zeros[:, :HEAD_DIM - ROT_DIM]], axis=1)
    rep = LANES // HEAD_DIM
    return (jnp.tile(c, (1, rep)), jnp.tile(s1, (1, rep)), jnp.tile(s2, (1, rep)),
            cos.T, sin.T)


def kernel(x, meta_tokens, norm_pre_mix, norm_post_mix, w_in, b_in, attn_sinks, w_attn_proj,
           conv_dw_w, conv_dw_b, conv_ln_g, conv_ln_b, w_conv_proj, b_conv_proj, w_out,
           norm_pre_ffn, norm_post_ffn, w_up, ffn_dw_w, ffn_dw_b, w_down):
    bsz, seq, _ = x.shape
    assert seq % MIX_TILE == 0 and seq % FFN_TILE == 0
    scale = HEAD_DIM ** -0.5
    row = lambda v: v.reshape(1, -1).astype(F32)

    wi, bi = w_in[0], b_in[0]
    c_q, c_k, c_v, c_glu = ATTN_WIDTH, ATTN_WIDTH + KV_WIDTH, ATTN_WIDTH + 2 * KV_WIDTH, \
        ATTN_WIDTH + 2 * KV_WIDTH + 2 * CONV_CH
    wqT = (wi[:, :c_q] * scale).T.astype(BF16)
    bq = bi[:c_q] * scale
    wk = wi[:, c_q:c_k].astype(BF16)
    bk = row(bi[c_q:c_k])
    wvT = wi[:, c_k:c_v].T.astype(BF16)
    bv = bi[c_k:c_v]
    wglu = wi[:, c_v:c_glu].astype(BF16)
    bglu = row(bi[c_v:c_glu])
    wgate = wi[:, c_glu:].astype(BF16)
    bgate = row(bi[c_glu:])
    sinks = attn_sinks[0].astype(F32)
    wap = w_attn_proj[0].astype(BF16)
    dww = conv_dw_w[0].astype(F32)
    dww8 = jnp.broadcast_to(dww[:, None, :], (CONV_K, SUBLANES, CONV_CH))
    dwb = row(conv_dw_b[0])
    lng, lnb = row(conv_ln_g[0]), row(conv_ln_b[0])
    wcp = w_conv_proj[0].astype(BF16)
    bcp = row(b_conv_proj[0])
    wout = w_out[0].astype(BF16)
    gpre, gpost = row(norm_pre_mix[0]), row(norm_post_mix[0])
    gffn, gffn_post = row(norm_pre_ffn[0]), row(norm_post_ffn[0])

    nch = 2 * N_FFN_CHUNKS
    wup = w_up[0].astype(BF16)
    dwp = jnp.concatenate(
        [ffn_dw_w[0], ffn_dw_b[0][None, :], jnp.zeros((SUBLANES - FFN_CONV_K - 1, 2 * FFN_DIM), F32)],
        axis=0)
    dwp = dwp * jnp.concatenate([jnp.full((FFN_DIM,), 0.5, F32), jnp.ones((FFN_DIM,), F32)])
    wdown = w_down[0].astype(BF16)

    rc, rs1, rs2, cosT, sinT = _rope_tables(N_META + seq)

    smem = pl.BlockSpec(memory_space=pltpu.SMEM)
    vmem = pl.BlockSpec(memory_space=pltpu.VMEM)

    k_meta, v_meta, glu_meta, u_meta = pl.pallas_call(
        _meta_kernel,
        out_shape=(jax.ShapeDtypeStruct((N_META, KV_WIDTH), F32),
                   jax.ShapeDtypeStruct((N_META, KV_WIDTH), F32),
                   jax.ShapeDtypeStruct((N_META, CONV_CH), F32),
                   jax.ShapeDtypeStruct((N_META, 2 * FFN_DIM), F32)),
        in_specs=[vmem] * 12 + [smem] + [vmem] * 14,
        out_specs=(vmem, vmem, vmem, vmem),
        scratch_shapes=[pltpu.VMEM((CONV_HALO + N_META, CONV_CH), F32)],
        compiler_params=pltpu.CompilerParams(vmem_limit_bytes=VMEM_LIMIT),
        name="meta_prologue",
    )(meta_tokens.astype(F32), gpre, wqT, row(bq), wk, bk, wvT, row(bv), wglu, bglu, wgate,
      bgate, sinks, wap, dww, dwb, lng, lnb, wcp, bcp, wout, gpost, gffn, wup,
      rc[:N_META], rs1[:N_META], rs2[:N_META])

    vmetaT = v_meta.T.astype(BF16)
    umeta_tail = u_meta[N_META - SUBLANES:, :]

    T = MIX_TILE
    bqT = jnp.broadcast_to(bq[:, None], (ATTN_WIDTH, T)).astype(F32)
    bvT = jnp.broadcast_to(bv[:, None], (KV_WIDTH, T)).astype(F32)
    tile_spec = pl.BlockSpec((1, T, D_MODEL), lambda b, t: (b, t, 0))
    rope_spec = pl.BlockSpec((T, LANES), lambda b, t: (t, 0))
    ropeT_spec = pl.BlockSpec((SUBLANES, T), lambda b, t: (0, t))
    mixer_in = [
        (x, tile_spec), (gpre, None), (wqT, None), (bqT, None), (wk, None), (bk, None),
        (wvT, None), (bvT, None), (wglu, None), (bglu, None), (wgate, None), (bgate, None),
        (sinks, smem), (wap, None), (dww8, None), (dwb, None), (lng, None), (lnb, None),
        (wcp, None), (bcp, None), (wout, None), (gpost, None),
        (rc[N_META:], rope_spec), (rs1[N_META:], rope_spec), (rs2[N_META:], rope_spec),
        (cosT[:, N_META:], ropeT_spec), (sinT[:, N_META:], ropeT_spec),
        (k_meta, None), (vmetaT, None), (glu_meta, None),
    ]
    h2 = pl.pallas_call(
        _mixer_kernel,
        out_shape=jax.ShapeDtypeStruct((bsz, seq, D_MODEL), F32),
        grid=(bsz, seq // T),
        in_specs=[spec if spec is not None else _const_spec(a.shape) for a, spec in mixer_in],
        out_specs=tile_spec,
        scratch_shapes=[
            pltpu.VMEM((4, BLOCK + T, LANES), BF16),
            pltpu.VMEM((KV_WIDTH, BLOCK + T), BF16),
            pltpu.VMEM((CONV_HALO + T, CONV_CH), F32),
            pltpu.VMEM((CONV_SHIFTS - 1, T + CONV_PAD, CONV_CH), F32),
            pltpu.VMEM((ATTN_WIDTH, T), BF16),
            pltpu.VMEM((T // BLOCK, ATTN_WIDTH, BLOCK), F32),
            pltpu.VMEM((T, D_MODEL), F32),
            pltpu.VMEM((T, CONV_CH), F32),
            pltpu.VMEM((2, 2 * BLOCK, 2 * BLOCK), F32),
        ],
        compiler_params=pltpu.CompilerParams(
            dimension_semantics=("arbitrary", "arbitrary"), vmem_limit_bytes=VMEM_LIMIT),
        name="mixer",
    )(*[a for a, _ in mixer_in])

    T2 = FFN_TILE
    tile2 = pl.BlockSpec((1, T2, D_MODEL), lambda b, t: (b, t, 0))
    ffn_in = [(h2, tile2), (gffn, None), (gffn_post, None), (wup, None), (dwp, None),
              (wdown, None), (umeta_tail, None)]
    out = pl.pallas_call(
        _ffn_kernel,
        out_shape=jax.ShapeDtypeStruct((bsz, seq, D_MODEL), F32),
        grid=(bsz, seq // T2),
        in_specs=[spec if spec is not None else _const_spec(a.shape) for a, spec in ffn_in],
        out_specs=tile2,
        scratch_shapes=[
            pltpu.VMEM((T2, D_MODEL), BF16),
            pltpu.VMEM((FFN_UP_AHEAD + 1, 2, SUBLANES + T2, FFN_CHUNK), F32),
            pltpu.VMEM((SUBLANES, 2 * FFN_DIM), F32),
            pltpu.VMEM((T2, FFN_DIM), BF16),
            pltpu.VMEM((T2, D_MODEL), F32),
        ],
        compiler_params=pltpu.CompilerParams(
            dimension_semantics=("arbitrary", "arbitrary"), vmem_limit_bytes=VMEM_LIMIT),
        name="conv_ffn",
    )(*[a for a, _ in ffn_in])
    return out.astype(x.dtype)
```

```python
import jax
import jax.numpy as jnp
from jax import lax
from jax.experimental import pallas as pl
from jax.experimental.pallas import tpu as pltpu

D_MODEL = 1024
N_META = 16
N_Q_HEADS = 16
N_KV_HEADS = 2
HEAD_DIM = 64
GROUP = N_Q_HEADS // N_KV_HEADS
ROT_DIM = HEAD_DIM // 4
ROT_HALF = ROT_DIM // 2
ROPE_THETA = 500000.0
WINDOW = 128
BLOCK = 128
ATTN_WIDTH = N_Q_HEADS * HEAD_DIM
KV_WIDTH = N_KV_HEADS * HEAD_DIM
CONV_CH = D_MODEL
CONV_K = 31
FFN_DIM = 2816
FFN_CONV_K = 3
RMS_EPS = 1e-6
LN_EPS = 1e-5
NEG_INF = -1e30

LANES = 128
SUBLANES = 8
MIX_TILE = 512
FFN_TILE = 512
FFN_CHUNK = 256
N_FFN_CHUNKS = FFN_DIM // FFN_CHUNK
FFN_DOWN_GROUPS = ((0, 3), (3, 6), (6, 9), (9, 11))
FFN_UP_AHEAD = 2
CONV_HALO = 32
CONV_SHIFTS = SUBLANES
CONV_PAD = CONV_HALO - SUBLANES
CONV_ROWS = 32
CONV_COLS = 256
ATTN_LOOKAHEAD = 2
GATE_CHUNKS = 4
VMEM_LIMIT = 56 * 1024 * 1024

F32 = jnp.float32
BF16 = jnp.bfloat16

_NT = (((1,), (1,)), ((), ()))


def _dot(a, b):
    return jnp.dot(a, b, preferred_element_type=F32)


def _dot_nt(a, b):
    return lax.dot_general(a, b, _NT, preferred_element_type=F32)


def _rms(x, g):
    ms = jnp.mean(x * x, axis=-1, keepdims=True)
    return x * lax.rsqrt(ms + RMS_EPS) * g


def _layer_norm(x, g, b):
    mu = jnp.mean(x, axis=-1, keepdims=True)
    xc = x - mu
    var = jnp.mean(xc * xc, axis=-1, keepdims=True)
    return xc * lax.rsqrt(var + LN_EPS) * g + b


def _sigmoid(x):
    return 0.5 * jnp.tanh(0.5 * x) + 0.5


def _silu_2x(hx):
    return hx * jnp.tanh(hx) + hx


def _silu(x):
    return _silu_2x(0.5 * x)


def _rope_rows(z, c, s1, s2):
    parts = []
    for g in range(z.shape[1] // LANES):
        zg = z[:, g * LANES:(g + 1) * LANES]
        parts.append(zg * c + pltpu.roll(zg, LANES - ROT_HALF, 1) * s1
                     + pltpu.roll(zg, ROT_HALF, 1) * s2)
    return parts[0] if len(parts) == 1 else jnp.concatenate(parts, axis=1)


def _key_variants(k):
    lane = lax.broadcasted_iota(jnp.int32, k.shape, 1)
    lo = lane < HEAD_DIM
    kr = pltpu.roll(k, HEAD_DIM, 1)
    zero = jnp.zeros_like(k)
    return [jnp.where(lo, k, zero).astype(BF16), jnp.where(lo, zero, kr).astype(BF16),
            jnp.where(lo, kr, zero).astype(BF16), jnp.where(lo, zero, k).astype(BF16)]


def _meta_kernel(meta_ref, gpre_ref, wqT_ref, bq_ref, wk_ref, bk_ref, wvT_ref, bv_ref,
                 wglu_ref, bglu_ref, wgate_ref, bgate_ref, sink_ref, wap_ref, dww_ref,
                 dwb_ref, lng_ref, lnb_ref, wcp_ref, bcp_ref, wout_ref, gpost_ref,
                 gffn_ref, wup_ref, rc_ref, rs1_ref, rs2_ref,
                 k_out, v_out, glu_out, u_out, gscr):
    m = meta_ref[...]
    hn = _rms(m, gpre_ref[...]).astype(BF16)
    q = _dot_nt(hn, wqT_ref[...]) + bq_ref[...]
    k = _dot(hn, wk_ref[...]) + bk_ref[...]
    v = _dot_nt(hn, wvT_ref[...]) + bv_ref[...]
    rc, rs1, rs2 = rc_ref[...], rs1_ref[...], rs2_ref[...]
    q = _rope_rows(q, rc, rs1, rs2)
    k = _rope_rows(k, rc, rs1, rs2)
    k_out[...] = k
    v_out[...] = v

    row = lax.broadcasted_iota(jnp.int32, (N_META, N_META), 0)
    col = lax.broadcasted_iota(jnp.int32, (N_META, N_META), 1)
    causal = col <= row
    qb, kb, vb = q.astype(BF16), k.astype(BF16), v.astype(BF16)
    attn = jnp.zeros((N_META, D_MODEL), F32)
    for hq in range(N_Q_HEADS):
        h = hq // GROUP
        qh = qb[:, hq * HEAD_DIM:(hq + 1) * HEAD_DIM]
        kh = kb[:, h * HEAD_DIM:(h + 1) * HEAD_DIM]
        vh = vb[:, h * HEAD_DIM:(h + 1) * HEAD_DIM]
        s = jnp.where(causal, _dot_nt(qh, kh), NEG_INF)
        sink = sink_ref[hq]
        mx = jnp.maximum(jnp.max(s, axis=-1, keepdims=True), sink)
        p = jnp.exp(s - mx)
        den = jnp.sum(p, axis=-1, keepdims=True) + jnp.exp(sink - mx)
        o = _dot(p.astype(BF16), vh) / den
        attn = attn + _dot(o.astype(BF16), wap_ref[hq * HEAD_DIM:(hq + 1) * HEAD_DIM, :])

    glu_in = _dot(hn, wglu_ref[...]) + bglu_ref[...]
    glu = glu_in[:, :CONV_CH] * _sigmoid(glu_in[:, CONV_CH:])
    glu_out[...] = glu
    gscr[0:CONV_HALO, :] = jnp.zeros((CONV_HALO, CONV_CH), F32)
    gscr[CONV_HALO:CONV_HALO + N_META, :] = glu
    c = jnp.broadcast_to(dwb_ref[...], (N_META, CONV_CH))
    off = CONV_HALO - (CONV_K - 1)
    for kk in range(CONV_K):
        c = c + gscr[off + kk:off + kk + N_META, :] * dww_ref[kk:kk + 1, :]
    c = _silu(_layer_norm(c, lng_ref[...], lnb_ref[...]))
    conv = _dot(c.astype(BF16), wcp_ref[...]) + bcp_ref[...]

    gates = _sigmoid(_dot(hn, wgate_ref[...]) + bgate_ref[...])
    merged = gates[:, :D_MODEL] * attn + gates[:, D_MODEL:] * conv
    mix = _dot(merged.astype(BF16), wout_ref[...])
    h2 = m + _rms(mix, gpost_ref[...])
    hn2 = _rms(h2, gffn_ref[...]).astype(BF16)
    u_out[...] = _dot(hn2, wup_ref[...])


def _mixer_kernel(x_ref, gpre_ref, wqT_ref, bqT_ref, wk_ref, bk_ref, wvT_ref, bvT_ref,
                  wglu_ref, bglu_ref, wgate_ref, bgate_ref, sink_ref, wap_ref, dww_ref,
                  dwb_ref, lng_ref, lnb_ref, wcp_ref, bcp_ref, wout_ref, gpost_ref,
                  rc_ref, rs1_ref, rs2_ref, cosT_ref, sinT_ref,
                  kmeta_ref, vmetaT_ref, glumeta_ref,
                  out_ref,
                  kbuf, vTbuf, gbuf, shbuf, qT_s, oT_s, attn_s, cbuf, bias_s):
    T = MIX_TILE
    nblk = T // BLOCK
    t = pl.program_id(1)

    @pl.when(t == 0)
    def _init():
        kbuf[:, 0:BLOCK, :] = jnp.zeros((4, BLOCK, LANES), BF16)
        vTbuf[:, 0:BLOCK] = jnp.zeros((KV_WIDTH, BLOCK), BF16)
        gbuf[0:CONV_HALO - N_META, :] = jnp.zeros((CONV_HALO - N_META, CONV_CH), F32)
        gbuf[CONV_HALO - N_META:CONV_HALO, :] = glumeta_ref[...]
        key = lax.broadcasted_iota(jnp.int32, (2 * BLOCK, 2 * BLOCK), 0)
        qry = lax.broadcasted_iota(jnp.int32, (2 * BLOCK, 2 * BLOCK), 1) % BLOCK
        vis = (key > qry) & (key <= qry + WINDOW)
        bias_s[0] = jnp.where(vis, 0.0, NEG_INF).astype(F32)
        bias_s[1] = jnp.where(vis & (key >= BLOCK), 0.0, NEG_INF).astype(F32)

    x = x_ref[0]
    hn = _rms(x, gpre_ref[...]).astype(BF16)

    glu_in = _dot(hn, wglu_ref[...]) + bglu_ref[...]
    gbuf[CONV_HALO:CONV_HALO + T, :] = glu_in[:, :CONV_CH] * _sigmoid(glu_in[:, CONV_CH:])
    for s in range(1, CONV_SHIFTS):
        shbuf[s - 1] = gbuf[SUBLANES - s:SUBLANES - s + T + CONV_PAD, :]

    qT = _dot_nt(wqT_ref[...], hn) + bqT_ref[...]
    cosT, sinT = cosT_ref[...], sinT_ref[...]
    for hq in range(N_Q_HEADS):
        base = hq * HEAD_DIM
        r0 = qT[base:base + ROT_HALF]
        r1 = qT[base + ROT_HALF:base + ROT_DIM]
        rot = jnp.concatenate([r0 * cosT - r1 * sinT, r1 * cosT + r0 * sinT], axis=0)
        qT_s[base:base + ROT_DIM, :] = rot.astype(BF16)
        qT_s[base + ROT_DIM:base + HEAD_DIM, :] = qT[base + ROT_DIM:base + HEAD_DIM].astype(BF16)

    k = _dot(hn, wk_ref[...]) + bk_ref[...]
    k = _rope_rows(k, rc_ref[...], rs1_ref[...], rs2_ref[...])
    for i, kv in enumerate(_key_variants(k)):
        kbuf[i, BLOCK:BLOCK + T, :] = kv
    kmeta_var = _key_variants(kmeta_ref[...])
    vT = _dot_nt(wvT_ref[...], hn) + bvT_ref[...]
    vTbuf[:, BLOCK:BLOCK + T] = vT.astype(BF16)
    vmetaT = vmetaT_ref[...]

    lane2 = lax.broadcasted_iota(jnp.int32, (1, 2 * BLOCK), 1)
    first_sel = jnp.where(t == 0, 1, 0)

    items = [(j, gp, hl) for j in range(nblk) for gp in range(4) for hl in range(2)]

    def scores(item):
        j, gp, hl = item
        h = gp // 2
        c0 = j * BLOCK
        rhs = jnp.concatenate(
            [qT_s[256 * gp:256 * gp + 128, c0:c0 + BLOCK],
             qT_s[256 * gp + 128:256 * gp + 256, c0:c0 + BLOCK]], axis=1)
        s_loc = _dot(kbuf[2 * h + hl, c0:c0 + 2 * BLOCK, :], rhs)
        s_met = _dot(kmeta_var[2 * h + hl], rhs)
        return s_loc, s_met

    def softmax(item, s_loc, s_met):
        j, gp, hl = item
        ha = 4 * gp + hl
        hb = ha + 2
        s_loc = s_loc + (bias_s[first_sel] if j == 0 else bias_s[0])
        sink = jnp.where(lane2 < BLOCK, sink_ref[ha], sink_ref[hb])
        mx = jnp.maximum(jnp.maximum(jnp.max(s_loc, axis=0, keepdims=True),
                                     jnp.max(s_met, axis=0, keepdims=True)), sink)
        p_loc = jnp.exp(s_loc - mx)
        p_met = jnp.exp(s_met - mx)
        den = (jnp.sum(p_loc, axis=0, keepdims=True)
               + jnp.sum(p_met, axis=0, keepdims=True) + jnp.exp(sink - mx))
        return p_loc.astype(BF16), p_met.astype(BF16), 1.0 / den

    def weighted_values(item, p_loc, p_met, inv_den):
        j, gp, hl = item
        h = gp // 2
        c0 = j * BLOCK
        ha = 4 * gp + hl
        hb = ha + 2
        vT_h = vTbuf[h * HEAD_DIM:(h + 1) * HEAD_DIM, c0:c0 + 2 * BLOCK]
        vmT_h = vmetaT[h * HEAD_DIM:(h + 1) * HEAD_DIM, :]
        oT = (_dot(vT_h, p_loc) + _dot(vmT_h, p_met)) * inv_den
        oT_s[j, ha * HEAD_DIM:(ha + 1) * HEAD_DIM, :] = oT[:, :BLOCK]
        oT_s[j, hb * HEAD_DIM:(hb + 1) * HEAD_DIM, :] = oT[:, BLOCK:]

    def conv_block(cc, rc):
        cols = slice(cc * CONV_COLS, (cc + 1) * CONV_COLS)
        r0 = rc * CONV_ROWS
        nsub = CONV_ROWS // SUBLANES
        bias8 = jnp.broadcast_to(dwb_ref[:, cols], (SUBLANES, CONV_COLS))
        accs = [bias8] * nsub
        for d in range(CONV_K):
            a, s = divmod(d, SUBLANES)
            w = dww_ref[CONV_K - 1 - d, :, cols]
            for r in range(nsub):
                if s == 0:
                    lo = r0 + r * SUBLANES + CONV_HALO - SUBLANES * a
                    g = gbuf[lo:lo + SUBLANES, cols]
                else:
                    lo = r0 + r * SUBLANES + CONV_PAD - SUBLANES * a
                    g = shbuf[s - 1, lo:lo + SUBLANES, cols]
                accs[r] = accs[r] + g * w
        for r in range(nsub):
            cbuf[r0 + r * SUBLANES:r0 + (r + 1) * SUBLANES, cols] = accs[r]

    conv_blocks = [(cc, rc) for cc in range(CONV_CH // CONV_COLS) for rc in range(T // CONV_ROWS)]
    conv_per_item = -(-len(conv_blocks) // len(items))

    n_items = len(items)
    items_per_blk = n_items // nblk
    items_per_gate = n_items // GATE_CHUNKS
    gate_w = 2 * D_MODEL // GATE_CHUNKS
    pending = [scores(it) for it in items[:ATTN_LOOKAHEAD]]
    gate_parts = []
    for i, it in enumerate(items):
        if i + ATTN_LOOKAHEAD < n_items:
            pending.append(scores(items[i + ATTN_LOOKAHEAD]))
        weighted_values(it, *softmax(it, *pending.pop(0)))
        for _ in range(conv_per_item):
            if conv_blocks:
                conv_block(*conv_blocks.pop(0))
        if (i + 1) % items_per_gate == 0:
            gcols = slice(i // items_per_gate * gate_w, (i // items_per_gate + 1) * gate_w)
            gate_parts.append(_sigmoid(_dot(hn, wgate_ref[:, gcols]) + bgate_ref[:, gcols]))
        if (i + 1) % items_per_blk == 0:
            j = i // items_per_blk
            o_blk = oT_s[j].T.astype(BF16)
            attn_s[j * BLOCK:(j + 1) * BLOCK, :] = _dot(o_blk, wap_ref[...])
    while conv_blocks:
        conv_block(*conv_blocks.pop(0))
    gates = jnp.concatenate(gate_parts, axis=1)

    kbuf[:, 0:BLOCK, :] = kbuf[:, T:T + BLOCK, :]
    vTbuf[:, 0:BLOCK] = vTbuf[:, T:T + BLOCK]
    gbuf[0:CONV_HALO, :] = gbuf[T:T + CONV_HALO, :]

    c = _silu(_layer_norm(cbuf[...], lng_ref[...], lnb_ref[...]))
    conv = _dot(c.astype(BF16), wcp_ref[...]) + bcp_ref[...]

    merged = gates[:, :D_MODEL] * attn_s[...] + gates[:, D_MODEL:] * conv
    mix = _dot(merged.astype(BF16), wout_ref[...])
    out_ref[0] = x + _rms(mix, gpost_ref[...])


def _ffn_cols(cidx, half):
    start = half * FFN_DIM + cidx * FFN_CHUNK
    return slice(start, start + FFN_CHUNK)


def _ffn_kernel(h_ref, gpre_ref, gpost_ref, wup_ref, dwp_ref, wdown_ref, umeta_ref,
                out_ref, hn_s, ubuf, carry, act_s, acc_s):
    T = FFN_TILE
    t = pl.program_id(1)

    @pl.when(t == 0)
    def _init():
        carry[...] = umeta_ref[...]

    h = h_ref[0]
    hn_s[...] = _rms(h, gpre_ref[...]).astype(BF16)

    def up(cidx):
        par = cidx % (FFN_UP_AHEAD + 1)
        for half in range(2):
            cols = _ffn_cols(cidx, half)
            u = _dot(hn_s[...], wup_ref[:, cols])
            ubuf[par, half, 0:SUBLANES, :] = carry[:, cols]
            ubuf[par, half, SUBLANES:SUBLANES + T, :] = u
            carry[:, cols] = u[T - SUBLANES:T, :]

    def activate(cidx):
        par = cidx % (FFN_UP_AHEAD + 1)
        ys = []
        for half in range(2):
            w = dwp_ref[:, _ffn_cols(cidx, half)]
            ue = ubuf[par, half]
            delayed = [pltpu.roll(ue, d, 0)[SUBLANES:SUBLANES + T, :] if d else ue[SUBLANES:SUBLANES + T, :]
                       for d in range(FFN_CONV_K)]
            ys.append(w[3:4, :] + w[0:1, :] * delayed[2] + w[1:2, :] * delayed[1]
                      + w[2:3, :] * delayed[0])
        act_s[:, cidx * FFN_CHUNK:(cidx + 1) * FFN_CHUNK] = (_silu_2x(ys[0]) * ys[1]).astype(BF16)

    total = None
    for cidx in range(FFN_UP_AHEAD):
        up(cidx)
    for cidx in range(N_FFN_CHUNKS):
        if cidx + FFN_UP_AHEAD < N_FFN_CHUNKS:
            up(cidx + FFN_UP_AHEAD)
        activate(cidx)
        for lo, hi in FFN_DOWN_GROUPS:
            if cidx == hi - 1:
                part = _dot(act_s[:, lo * FFN_CHUNK:hi * FFN_CHUNK],
                            wdown_ref[lo * FFN_CHUNK:hi * FFN_CHUNK, :])
                if hi == N_FFN_CHUNKS:
                    total = acc_s[...] + part
                elif lo == 0:
                    acc_s[...] = part
                else:
                    acc_s[...] += part
    out_ref[0] = h + _rms(total, gpost_ref[...])


def _const_spec(shape):
    nd = len(shape)
    return pl.BlockSpec(shape, lambda *_: (0,) * nd, pipeline_mode=pl.Buffered(1))


def _rope_tables(n_pos):
    inv_freq = ROPE_THETA ** (-jnp.arange(ROT_HALF, dtype=F32) * 2.0 / ROT_DIM)
    ang = jnp.arange(n_pos).astype(F32)[:, None] * inv_freq[None, :]
    cos, sin = jnp.cos(ang), jnp.sin(ang)
    ones = jnp.ones((n_pos, HEAD_DIM - ROT_DIM), F32)
    zeros = jnp.zeros((n_pos, HEAD_DIM - ROT_HALF), F32)
    c = jnp.concatenate([cos, cos, ones], axis=1)
    s1 = jnp.concatenate([-sin, zeros], axis=1)
    s2 = jnp.concatenate([zeros[:, :ROT_HALF], sin, zeros[:, :HEAD_DIM - ROT_DIM]], axis=1)
    rep = LANES // HEAD_DIM
    return (jnp.tile(c, (1, rep)), jnp.tile(s1, (1, rep)), jnp.tile(s2, (1, rep)),
            cos.T, sin.T)


def kernel(x, meta_tokens, norm_pre_mix, norm_post_mix, w_in, b_in, attn_sinks, w_attn_proj,
           conv_dw_w, conv_dw_b, conv_ln_g, conv_ln_b, w_conv_proj, b_conv_proj, w_out,
           norm_pre_ffn, norm_post_ffn, w_up, ffn_dw_w, ffn_dw_b, w_down):
    bsz, seq, _ = x.shape
    assert seq % MIX_TILE == 0 and seq % FFN_TILE == 0
    scale = HEAD_DIM ** -0.5
    row = lambda v: v.reshape(1, -1).astype(F32)

    wi, bi = w_in[0], b_in[0]
    c_q, c_k, c_v, c_glu = ATTN_WIDTH, ATTN_WIDTH + KV_WIDTH, ATTN_WIDTH + 2 * KV_WIDTH, \
        ATTN_WIDTH + 2 * KV_WIDTH + 2 * CONV_CH
    wqT = (wi[:, :c_q] * scale).T.astype(BF16)
    bq = bi[:c_q] * scale
    wk = wi[:, c_q:c_k].astype(BF16)
    bk = row(bi[c_q:c_k])
    wvT = wi[:, c_k:c_v].T.astype(BF16)
    bv = bi[c_k:c_v]
    wglu = wi[:, c_v:c_glu].astype(BF16)
    bglu = row(bi[c_v:c_glu])
    wgate = wi[:, c_glu:].astype(BF16)
    bgate = row(bi[c_glu:])
    sinks = attn_sinks[0].astype(F32)
    wap = w_attn_proj[0].astype(BF16)
    dww = conv_dw_w[0].astype(F32)
    dww8 = jnp.broadcast_to(dww[:, None, :], (CONV_K, SUBLANES, CONV_CH))
    dwb = row(conv_dw_b[0])
    lng, lnb = row(conv_ln_g[0]), row(conv_ln_b[0])
    wcp = w_conv_proj[0].astype(BF16)
    bcp = row(b_conv_proj[0])
    wout = w_out[0].astype(BF16)
    gpre, gpost = row(norm_pre_mix[0]), row(norm_post_mix[0])
    gffn, gffn_post = row(norm_pre_ffn[0]), row(norm_post_ffn[0])

    nch = 2 * N_FFN_CHUNKS
    wup = w_up[0].astype(BF16)
    dwp = jnp.concatenate(
        [ffn_dw_w[0], ffn_dw_b[0][None, :], jnp.zeros((SUBLANES - FFN_CONV_K - 1, 2 * FFN_DIM), F32)],
        axis=0)
    dwp = dwp * jnp.concatenate([jnp.full((FFN_DIM,), 0.5, F32), jnp.ones((FFN_DIM,), F32)])
    wdown = w_down[0].astype(BF16)

    rc, rs1, rs2, cosT, sinT = _rope_tables(N_META + seq)

    smem = pl.BlockSpec(memory_space=pltpu.SMEM)
    vmem = pl.BlockSpec(memory_space=pltpu.VMEM)

    k_meta, v_meta, glu_meta, u_meta = pl.pallas_call(
        _meta_kernel,
        out_shape=(jax.ShapeDtypeStruct((N_META, KV_WIDTH), F32),
                   jax.ShapeDtypeStruct((N_META, KV_WIDTH), F32),
                   jax.ShapeDtypeStruct((N_META, CONV_CH), F32),
                   jax.ShapeDtypeStruct((N_META, 2 * FFN_DIM), F32)),
        in_specs=[vmem] * 12 + [smem] + [vmem] * 14,
        out_specs=(vmem, vmem, vmem, vmem),
        scratch_shapes=[pltpu.VMEM((CONV_HALO + N_META, CONV_CH), F32)],
        compiler_params=pltpu.CompilerParams(vmem_limit_bytes=VMEM_LIMIT),
        name="meta_prologue",
    )(meta_tokens.astype(F32), gpre, wqT, row(bq), wk, bk, wvT, row(bv), wglu, bglu, wgate,
      bgate, sinks, wap, dww, dwb, lng, lnb, wcp, bcp, wout, gpost, gffn, wup,
      rc[:N_META], rs1[:N_META], rs2[:N_META])

    vmetaT = v_meta.T.astype(BF16)
    umeta_tail = u_meta[N_META - SUBLANES:, :]

    T = MIX_TILE
    bqT = jnp.broadcast_to(bq[:, None], (ATTN_WIDTH, T)).astype(F32)
    bvT = jnp.broadcast_to(bv[:, None], (KV_WIDTH, T)).astype(F32)
    tile_spec = pl.BlockSpec((1, T, D_MODEL), lambda b, t: (b, t, 0))
    rope_spec = pl.BlockSpec((T, LANES), lambda b, t: (t, 0))
    ropeT_spec = pl.BlockSpec((SUBLANES, T), lambda b, t: (0, t))
    mixer_in = [
        (x, tile_spec), (gpre, None), (wqT, None), (bqT, None), (wk, None), (bk, None),
        (wvT, None), (bvT, None), (wglu, None), (bglu, None), (wgate, None), (bgate, None),
        (sinks, smem), (wap, None), (dww8, None), (dwb, None), (lng, None), (lnb, None),
        (wcp, None), (bcp, None), (wout, None), (gpost, None),
        (rc[N_META:], rope_spec), (rs1[N_META:], rope_spec), (rs2[N_META:], rope_spec),
        (cosT[:, N_META:], ropeT_spec), (sinT[:, N_META:], ropeT_spec),
        (k_meta, None), (vmetaT, None), (glu_meta, None),
    ]
    h2 = pl.pallas_call(
        _mixer_kernel,
        out_shape=jax.ShapeDtypeStruct((bsz, seq, D_MODEL), F32),
        grid=(bsz, seq // T),
        in_specs=[spec if spec is not None else _const_spec(a.shape) for a, spec in mixer_in],
        out_specs=tile_spec,
        scratch_shapes=[
            pltpu.VMEM((4, BLOCK + T, LANES), BF16),
            pltpu.VMEM((KV_WIDTH, BLOCK + T), BF16),
            pltpu.VMEM((CONV_HALO + T, CONV_CH), F32),
            pltpu.VMEM((CONV_SHIFTS - 1, T + CONV_PAD, CONV_CH), F32),
            pltpu.VMEM((ATTN_WIDTH, T), BF16),
            pltpu.VMEM((T // BLOCK, ATTN_WIDTH, BLOCK), F32),
            pltpu.VMEM((T, D_MODEL), F32),
            pltpu.VMEM((T, CONV_CH), F32),
            pltpu.VMEM((2, 2 * BLOCK, 2 * BLOCK), F32),
        ],
        compiler_params=pltpu.CompilerParams(
            dimension_semantics=("arbitrary", "arbitrary"), vmem_limit_bytes=VMEM_LIMIT),
        name="mixer",
    )(*[a for a, _ in mixer_in])

    T2 = FFN_TILE
    tile2 = pl.BlockSpec((1, T2, D_MODEL), lambda b, t: (b, t, 0))
    ffn_in = [(h2, tile2), (gffn, None), (gffn_post, None), (wup, None), (dwp, None),
              (wdown, None), (umeta_tail, None)]
    out = pl.pallas_call(
        _ffn_kernel,
        out_shape=jax.ShapeDtypeStruct((bsz, seq, D_MODEL), F32),
        grid=(bsz, seq // T2),
        in_specs=[spec if spec is not None else _const_spec(a.shape) for a, spec in ffn_in],
        out_specs=tile2,
        scratch_shapes=[
            pltpu.VMEM((T2, D_MODEL), BF16),
            pltpu.VMEM((FFN_UP_AHEAD + 1, 2, SUBLANES + T2, FFN_CHUNK), F32),
            pltpu.VMEM((SUBLANES, 2 * FFN_DIM), F32),
            pltpu.VMEM((T2, FFN_DIM), BF16),
            pltpu.VMEM((T2, D_MODEL), F32),
        ],
        compiler_params=pltpu.CompilerParams(
            dimension_semantics=("arbitrary", "arbitrary"), vmem_limit_bytes=VMEM_LIMIT),
        name="conv_ffn",
    )(*[a for a, _ in ffn_in])
    return out.astype(x.dtype)
```

```python
import jax
import jax.numpy as jnp
from jax import lax
from jax.experimental import pallas as pl
from jax.experimental.pallas import tpu as pltpu

D_MODEL = 1024
N_META = 16
N_Q_HEADS = 16
N_KV_HEADS = 2
HEAD_DIM = 64
GROUP = N_Q_HEADS // N_KV_HEADS
ROT_DIM = HEAD_DIM // 4
ROT_HALF = ROT_DIM // 2
ROPE_THETA = 500000.0
WINDOW = 128
BLOCK = 128
ATTN_WIDTH = N_Q_HEADS * HEAD_DIM
KV_WIDTH = N_KV_HEADS * HEAD_DIM
CONV_CH = D_MODEL
CONV_K = 31
FFN_DIM = 2816
FFN_CONV_K = 3
RMS_EPS = 1e-6
LN_EPS = 1e-5
NEG_INF = -1e30

LANES = 128
SUBLANES = 8
MIX_TILE = 512
FFN_TILE = 512
FFN_CHUNK = 256
N_FFN_CHUNKS = FFN_DIM // FFN_CHUNK
FFN_DOWN_GROUPS = ((0, 4), (4, 8), (8, 11))
FFN_UP_AHEAD = 2
CONV_HALO = 32
CONV_SHIFTS = SUBLANES
CONV_PAD = CONV_HALO - SUBLANES
CONV_ROWS = 32
CONV_COLS = 256
ATTN_LOOKAHEAD = 2
GATE_CHUNKS = 4
VMEM_LIMIT = 56 * 1024 * 1024

F32 = jnp.float32
BF16 = jnp.bfloat16

_NT = (((1,), (1,)), ((), ()))


def _dot(a, b):
    return jnp.dot(a, b, preferred_element_type=F32)


def _dot_nt(a, b):
    return lax.dot_general(a, b, _NT, preferred_element_type=F32)


def _rms(x, g):
    ms = jnp.mean(x * x, axis=-1, keepdims=True)
    return x * lax.rsqrt(ms + RMS_EPS) * g


def _layer_norm(x, g, b):
    mu = jnp.mean(x, axis=-1, keepdims=True)
    xc = x - mu
    var = jnp.mean(xc * xc, axis=-1, keepdims=True)
    return xc * lax.rsqrt(var + LN_EPS) * g + b


def _sigmoid(x):
    return 0.5 * jnp.tanh(0.5 * x) + 0.5


def _silu_2x(hx):
    return hx * jnp.tanh(hx) + hx


def _silu(x):
    return _silu_2x(0.5 * x)


def _rope_rows(z, c, s1, s2):
    parts = []
    for g in range(z.shape[1] // LANES):
        zg = z[:, g * LANES:(g + 1) * LANES]
        parts.append(zg * c + pltpu.roll(zg, LANES - ROT_HALF, 1) * s1
                     + pltpu.roll(zg, ROT_HALF, 1) * s2)
    return parts[0] if len(parts) == 1 else jnp.concatenate(parts, axis=1)


def _key_variants(k):
    lane = lax.broadcasted_iota(jnp.int32, k.shape, 1)
    lo = lane < HEAD_DIM
    kr = pltpu.roll(k, HEAD_DIM, 1)
    zero = jnp.zeros_like(k)
    return [jnp.where(lo, k, zero).astype(BF16), jnp.where(lo, zero, kr).astype(BF16),
            jnp.where(lo, kr, zero).astype(BF16), jnp.where(lo, zero, k).astype(BF16)]


def _meta_kernel(meta_ref, gpre_ref, wqT_ref, bq_ref, wk_ref, bk_ref, wvT_ref, bv_ref,
                 wglu_ref, bglu_ref, wgate_ref, bgate_ref, sink_ref, wap_ref, dww_ref,
                 dwb_ref, lng_ref, lnb_ref, wcp_ref, bcp_ref, wout_ref, gpost_ref,
                 gffn_ref, wup_ref, rc_ref, rs1_ref, rs2_ref,
                 k_out, v_out, glu_out, u_out, gscr):
    m = meta_ref[...]
    hn = _rms(m, gpre_ref[...]).astype(BF16)
    q = _dot_nt(hn, wqT_ref[...]) + bq_ref[...]
    k = _dot(hn, wk_ref[...]) + bk_ref[...]
    v = _dot_nt(hn, wvT_ref[...]) + bv_ref[...]
    rc, rs1, rs2 = rc_ref[...], rs1_ref[...], rs2_ref[...]
    q = _rope_rows(q, rc, rs1, rs2)
    k = _rope_rows(k, rc, rs1, rs2)
    k_out[...] = k
    v_out[...] = v

    row = lax.broadcasted_iota(jnp.int32, (N_META, N_META), 0)
    col = lax.broadcasted_iota(jnp.int32, (N_META, N_META), 1)
    causal = col <= row
    qb, kb, vb = q.astype(BF16), k.astype(BF16), v.astype(BF16)
    attn = jnp.zeros((N_META, D_MODEL), F32)
    for hq in range(N_Q_HEADS):
        h = hq // GROUP
        qh = qb[:, hq * HEAD_DIM:(hq + 1) * HEAD_DIM]
        kh = kb[:, h * HEAD_DIM:(h + 1) * HEAD_DIM]
        vh = vb[:, h * HEAD_DIM:(h + 1) * HEAD_DIM]
        s = jnp.where(causal, _dot_nt(qh, kh), NEG_INF)
        sink = sink_ref[hq]
        mx = jnp.maximum(jnp.max(s, axis=-1, keepdims=True), sink)
        p = jnp.exp(s - mx)
        den = jnp.sum(p, axis=-1, keepdims=True) + jnp.exp(sink - mx)
        o = _dot(p.astype(BF16), vh) / den
        attn = attn + _dot(o.astype(BF16), wap_ref[hq * HEAD_DIM:(hq + 1) * HEAD_DIM, :])

    glu_in = _dot(hn, wglu_ref[...]) + bglu_ref[...]
    glu = glu_in[:, :CONV_CH] * _sigmoid(glu_in[:, CONV_CH:])
    glu_out[...] = glu
    gscr[0:CONV_HALO, :] = jnp.zeros((CONV_HALO, CONV_CH), F32)
    gscr[CONV_HALO:CONV_HALO + N_META, :] = glu
    c = jnp.broadcast_to(dwb_ref[...], (N_META, CONV_CH))
    off = CONV_HALO - (CONV_K - 1)
    for kk in range(CONV_K):
        c = c + gscr[off + kk:off + kk + N_META, :] * dww_ref[kk:kk + 1, :]
    c = _silu(_layer_norm(c, lng_ref[...], lnb_ref[...]))
    conv = _dot(c.astype(BF16), wcp_ref[...]) + bcp_ref[...]

    gates = _sigmoid(_dot(hn, wgate_ref[...]) + bgate_ref[...])
    merged = gates[:, :D_MODEL] * attn + gates[:, D_MODEL:] * conv
    mix = _dot(merged.astype(BF16), wout_ref[...])
    h2 = m + _rms(mix, gpost_ref[...])
    hn2 = _rms(h2, gffn_ref[...]).astype(BF16)
    u_out[...] = _dot(hn2, wup_ref[...])


def _mixer_kernel(x_ref, gpre_ref, wqT_ref, bqT_ref, wk_ref, bk_ref, wvT_ref, bvT_ref,
                  wglu_ref, bglu_ref, wgate_ref, bgate_ref, sink_ref, wap_ref, dww_ref,
                  dwb_ref, lng_ref, lnb_ref, wcp_ref, bcp_ref, wout_ref, gpost_ref,
                  rc_ref, rs1_ref, rs2_ref, cosT_ref, sinT_ref,
                  kmeta_ref, vmetaT_ref, glumeta_ref,
                  out_ref,
                  kbuf, vTbuf, gbuf, shbuf, qT_s, oT_s, attn_s, cbuf, bias_s):
    T = MIX_TILE
    nblk = T // BLOCK
    t = pl.program_id(1)

    @pl.when(t == 0)
    def _init():
        kbuf[:, 0:BLOCK, :] = jnp.zeros((4, BLOCK, LANES), BF16)
        vTbuf[:, 0:BLOCK] = jnp.zeros((KV_WIDTH, BLOCK), BF16)
        gbuf[0:CONV_HALO - N_META, :] = jnp.zeros((CONV_HALO - N_META, CONV_CH), F32)
        gbuf[CONV_HALO - N_META:CONV_HALO, :] = glumeta_ref[...]
        key = lax.broadcasted_iota(jnp.int32, (2 * BLOCK, 2 * BLOCK), 0)
        qry = lax.broadcasted_iota(jnp.int32, (2 * BLOCK, 2 * BLOCK), 1) % BLOCK
        vis = (key > qry) & (key <= qry + WINDOW)
        bias_s[0] = jnp.where(vis, 0.0, NEG_INF).astype(F32)
        bias_s[1] = jnp.where(vis & (key >= BLOCK), 0.0, NEG_INF).astype(F32)

    x = x_ref[0]
    hn = _rms(x, gpre_ref[...]).astype(BF16)

    glu_in = _dot(hn, wglu_ref[...]) + bglu_ref[...]
    gbuf[CONV_HALO:CONV_HALO + T, :] = glu_in[:, :CONV_CH] * _sigmoid(glu_in[:, CONV_CH:])
    for s in range(1, CONV_SHIFTS):
        shbuf[s - 1] = gbuf[SUBLANES - s:SUBLANES - s + T + CONV_PAD, :]

    qT = _dot_nt(wqT_ref[...], hn) + bqT_ref[...]
    cosT, sinT = cosT_ref[...], sinT_ref[...]
    for hq in range(N_Q_HEADS):
        base = hq * HEAD_DIM
        r0 = qT[base:base + ROT_HALF]
        r1 = qT[base + ROT_HALF:base + ROT_DIM]
        rot = jnp.concatenate([r0 * cosT - r1 * sinT, r1 * cosT + r0 * sinT], axis=0)
        qT_s[base:base + ROT_DIM, :] = rot.astype(BF16)
        qT_s[base + ROT_DIM:base + HEAD_DIM, :] = qT[base + ROT_DIM:base + HEAD_DIM].astype(BF16)

    k = _dot(hn, wk_ref[...]) + bk_ref[...]
    k = _rope_rows(k, rc_ref[...], rs1_ref[...], rs2_ref[...])
    for i, kv in enumerate(_key_variants(k)):
        kbuf[i, BLOCK:BLOCK + T, :] = kv
    kmeta_var = _key_variants(kmeta_ref[...])
    vT = _dot_nt(wvT_ref[...], hn) + bvT_ref[...]
    vTbuf[:, BLOCK:BLOCK + T] = vT.astype(BF16)
    vmetaT = vmetaT_ref[...]

    lane2 = lax.broadcasted_iota(jnp.int32, (1, 2 * BLOCK), 1)
    first_sel = jnp.where(t == 0, 1, 0)

    items = [(j, gp, hl) for j in range(nblk) for gp in range(4) for hl in range(2)]

    def scores(item):
        j, gp, hl = item
        h = gp // 2
        c0 = j * BLOCK
        rhs = jnp.concatenate(
            [qT_s[256 * gp:256 * gp + 128, c0:c0 + BLOCK],
             qT_s[256 * gp + 128:256 * gp + 256, c0:c0 + BLOCK]], axis=1)
        s_loc = _dot(kbuf[2 * h + hl, c0:c0 + 2 * BLOCK, :], rhs)
        s_met = _dot(kmeta_var[2 * h + hl], rhs)
        return s_loc, s_met

    def softmax(item, s_loc, s_met):
        j, gp, hl = item
        ha = 4 * gp + hl
        hb = ha + 2
        s_loc = s_loc + (bias_s[first_sel] if j == 0 else bias_s[0])
        sink = jnp.where(lane2 < BLOCK, sink_ref[ha], sink_ref[hb])
        mx = jnp.maximum(jnp.maximum(jnp.max(s_loc, axis=0, keepdims=True),
                                     jnp.max(s_met, axis=0, keepdims=True)), sink)
        p_loc = jnp.exp(s_loc - mx)
        p_met = jnp.exp(s_met - mx)
        den = (jnp.sum(p_loc, axis=0, keepdims=True)
               + jnp.sum(p_met, axis=0, keepdims=True) + jnp.exp(sink - mx))
        return p_loc.astype(BF16), p_met.astype(BF16), 1.0 / den

    def weighted_values(item, p_loc, p_met, inv_den):
        j, gp, hl = item
        h = gp // 2
        c0 = j * BLOCK
        ha = 4 * gp + hl
        hb = ha + 2
        vT_h = vTbuf[h * HEAD_DIM:(h + 1) * HEAD_DIM, c0:c0 + 2 * BLOCK]
        vmT_h = vmetaT[h * HEAD_DIM:(h + 1) * HEAD_DIM, :]
        oT = (_dot(vT_h, p_loc) + _dot(vmT_h, p_met)) * inv_den
        oT_s[j, ha * HEAD_DIM:(ha + 1) * HEAD_DIM, :] = oT[:, :BLOCK]
        oT_s[j, hb * HEAD_DIM:(hb + 1) * HEAD_DIM, :] = oT[:, BLOCK:]

    def conv_block(cc, rc):
        cols = slice(cc * CONV_COLS, (cc + 1) * CONV_COLS)
        r0 = rc * CONV_ROWS
        nsub = CONV_ROWS // SUBLANES
        bias8 = jnp.broadcast_to(dwb_ref[:, cols], (SUBLANES, CONV_COLS))
        accs = [bias8] * nsub
        for d in range(CONV_K):
            a, s = divmod(d, SUBLANES)
            w = dww_ref[CONV_K - 1 - d, :, cols]
            for r in range(nsub):
                if s == 0:
                    lo = r0 + r * SUBLANES + CONV_HALO - SUBLANES * a
                    g = gbuf[lo:lo + SUBLANES, cols]
                else:
                    lo = r0 + r * SUBLANES + CONV_PAD - SUBLANES * a
                    g = shbuf[s - 1, lo:lo + SUBLANES, cols]
                accs[r] = accs[r] + g * w
        for r in range(nsub):
            cbuf[r0 + r * SUBLANES:r0 + (r + 1) * SUBLANES, cols] = accs[r]

    conv_blocks = [(cc, rc) for cc in range(CONV_CH // CONV_COLS) for rc in range(T // CONV_ROWS)]
    conv_per_item = -(-len(conv_blocks) // len(items))

    n_items = len(items)
    items_per_blk = n_items // nblk
    items_per_gate = n_items // GATE_CHUNKS
    gate_w = 2 * D_MODEL // GATE_CHUNKS
    pending = [scores(it) for it in items[:ATTN_LOOKAHEAD]]
    gate_parts = []
    for i, it in enumerate(items):
        if i + ATTN_LOOKAHEAD < n_items:
            pending.append(scores(items[i + ATTN_LOOKAHEAD]))
        weighted_values(it, *softmax(it, *pending.pop(0)))
        for _ in range(conv_per_item):
            if conv_blocks:
                conv_block(*conv_blocks.pop(0))
        if (i + 1) % items_per_gate == 0:
            gcols = slice(i // items_per_gate * gate_w, (i // items_per_gate + 1) * gate_w)
            gate_parts.append(_sigmoid(_dot(hn, wgate_ref[:, gcols]) + bgate_ref[:, gcols]))
        if (i + 1) % items_per_blk == 0:
            j = i // items_per_blk
            o_blk = oT_s[j].T.astype(BF16)
            attn_s[j * BLOCK:(j + 1) * BLOCK, :] = _dot(o_blk, wap_ref[...])
    while conv_blocks:
        conv_block(*conv_blocks.pop(0))
    gates = jnp.concatenate(gate_parts, axis=1)

    kbuf[:, 0:BLOCK, :] = kbuf[:, T:T + BLOCK, :]
    vTbuf[:, 0:BLOCK] = vTbuf[:, T:T + BLOCK]
    gbuf[0:CONV_HALO, :] = gbuf[T:T + CONV_HALO, :]

    c = _silu(_layer_norm(cbuf[...], lng_ref[...], lnb_ref[...]))
    conv = _dot(c.astype(BF16), wcp_ref[...]) + bcp_ref[...]

    merged = gates[:, :D_MODEL] * attn_s[...] + gates[:, D_MODEL:] * conv
    mix = _dot(merged.astype(BF16), wout_ref[...])
    out_ref[0] = x + _rms(mix, gpost_ref[...])


def _ffn_cols(cidx, half):
    start = half * FFN_DIM + cidx * FFN_CHUNK
    return slice(start, start + FFN_CHUNK)


def _ffn_kernel(h_ref, gpre_ref, gpost_ref, wup_ref, dwp_ref, wdown_ref, umeta_ref,
                out_ref, hn_s, ubuf, carry, act_s, acc_s):
    T = FFN_TILE
    t = pl.program_id(1)

    @pl.when(t == 0)
    def _init():
        carry[...] = umeta_ref[...]

    h = h_ref[0]
    hn_s[...] = _rms(h, gpre_ref[...]).astype(BF16)

    def up(cidx):
        par = cidx % (FFN_UP_AHEAD + 1)
        for half in range(2):
            cols = _ffn_cols(cidx, half)
            u = _dot(hn_s[...], wup_ref[:, cols])
            ubuf[par, half, 0:SUBLANES, :] = carry[:, cols]
            ubuf[par, half, SUBLANES:SUBLANES + T, :] = u
            carry[:, cols] = u[T - SUBLANES:T, :]

    def activate(cidx):
        par = cidx % (FFN_UP_AHEAD + 1)
        ys = []
        for half in range(2):
            w = dwp_ref[:, _ffn_cols(cidx, half)]
            ue = ubuf[par, half]
            delayed = [pltpu.roll(ue, d, 0)[SUBLANES:SUBLANES + T, :] if d else ue[SUBLANES:SUBLANES + T, :]
                       for d in range(FFN_CONV_K)]
            ys.append(w[3:4, :] + w[0:1, :] * delayed[2] + w[1:2, :] * delayed[1]
                      + w[2:3, :] * delayed[0])
        act_s[:, cidx * FFN_CHUNK:(cidx + 1) * FFN_CHUNK] = (_silu_2x(ys[0]) * ys[1]).astype(BF16)

    total = None
    for cidx in range(FFN_UP_AHEAD):
        up(cidx)
    for cidx in range(N_FFN_CHUNKS):
        if cidx + FFN_UP_AHEAD < N_FFN_CHUNKS:
            up(cidx + FFN_UP_AHEAD)
        activate(cidx)
        for lo, hi in FFN_DOWN_GROUPS:
            if cidx == hi - 1:
                part = _dot(act_s[:, lo * FFN_CHUNK:hi * FFN_CHUNK],
                            wdown_ref[lo * FFN_CHUNK:hi * FFN_CHUNK, :])
                if hi == N_FFN_CHUNKS:
                    total = acc_s[...] + part
                elif lo == 0:
                    acc_s[...] = part
                else:
                    acc_s[...] += part
    out_ref[0] = h + _rms(total, gpost_ref[...])


def _const_spec(shape):
    nd = len(shape)
    return pl.BlockSpec(shape, lambda *_: (0,) * nd, pipeline_mode=pl.Buffered(1))


def _rope_tables(n_pos):
    inv_freq = ROPE_THETA ** (-jnp.arange(ROT_HALF, dtype=F32) * 2.0 / ROT_DIM)
    ang = jnp.arange(n_pos).astype(F32)[:, None] * inv_freq[None, :]
    cos, sin = jnp.cos(ang), jnp.sin(ang)
    ones = jnp.ones((n_pos, HEAD_DIM - ROT_DIM), F32)
    zeros = jnp.zeros((n_pos, HEAD_DIM - ROT_HALF), F32)
    c = jnp.concatenate([cos, cos, ones], axis=1)
    s1 = jnp.concatenate([-sin, zeros], axis=1)
    s2 = jnp.concatenate([zeros[:, :ROT_HALF], sin, zeros[:, :HEAD_DIM - ROT_DIM]], axis=1)
    rep = LANES // HEAD_DIM
    return (jnp.tile(c, (1, rep)), jnp.tile(s1, (1, rep)), jnp.tile(s2, (1, rep)),
            cos.T, sin.T)


def kernel(x, meta_tokens, norm_pre_mix, norm_post_mix, w_in, b_in, attn_sinks, w_attn_proj,
           conv_dw_w, conv_dw_b, conv_ln_g, conv_ln_b, w_conv_proj, b_conv_proj, w_out,
           norm_pre_ffn, norm_post_ffn, w_up, ffn_dw_w, ffn_dw_b, w_down):
    bsz, seq, _ = x.shape
    assert seq % MIX_TILE == 0 and seq % FFN_TILE == 0
    scale = HEAD_DIM ** -0.5
    row = lambda v: v.reshape(1, -1).astype(F32)

    wi, bi = w_in[0], b_in[0]
    c_q, c_k, c_v, c_glu = ATTN_WIDTH, ATTN_WIDTH + KV_WIDTH, ATTN_WIDTH + 2 * KV_WIDTH, \
        ATTN_WIDTH + 2 * KV_WIDTH + 2 * CONV_CH
    wqT = (wi[:, :c_q] * scale).T.astype(BF16)
    bq = bi[:c_q] * scale
    wk = wi[:, c_q:c_k].astype(BF16)
    bk = row(bi[c_q:c_k])
    wvT = wi[:, c_k:c_v].T.astype(BF16)
    bv = bi[c_k:c_v]
    wglu = wi[:, c_v:c_glu].astype(BF16)
    bglu = row(bi[c_v:c_glu])
    wgate = wi[:, c_glu:].astype(BF16)
    bgate = row(bi[c_glu:])
    sinks = attn_sinks[0].astype(F32)
    wap = w_attn_proj[0].astype(BF16)
    dww = conv_dw_w[0].astype(F32)
    dww8 = jnp.broadcast_to(dww[:, None, :], (CONV_K, SUBLANES, CONV_CH))
    dwb = row(conv_dw_b[0])
    lng, lnb = row(conv_ln_g[0]), row(conv_ln_b[0])
    wcp = w_conv_proj[0].astype(BF16)
    bcp = row(b_conv_proj[0])
    wout = w_out[0].astype(BF16)
    gpre, gpost = row(norm_pre_mix[0]), row(norm_post_mix[0])
    gffn, gffn_post = row(norm_pre_ffn[0]), row(norm_post_ffn[0])

    nch = 2 * N_FFN_CHUNKS
    wup = w_up[0].astype(BF16)
    dwp = jnp.concatenate(
        [ffn_dw_w[0], ffn_dw_b[0][None, :], jnp.zeros((SUBLANES - FFN_CONV_K - 1, 2 * FFN_DIM), F32)],
        axis=0)
    dwp = dwp * jnp.concatenate([jnp.full((FFN_DIM,), 0.5, F32), jnp.ones((FFN_DIM,), F32)])
    wdown = w_down[0].astype(BF16)

    rc, rs1, rs2, cosT, sinT = _rope_tables(N_META + seq)

    smem = pl.BlockSpec(memory_space=pltpu.SMEM)
    vmem = pl.BlockSpec(memory_space=pltpu.VMEM)

    k_meta, v_meta, glu_meta, u_meta = pl.pallas_call(
        _meta_kernel,
        out_shape=(jax.ShapeDtypeStruct((N_META, KV_WIDTH), F32),
                   jax.ShapeDtypeStruct((N_META, KV_WIDTH), F32),
                   jax.ShapeDtypeStruct((N_META, CONV_CH), F32),
                   jax.ShapeDtypeStruct((N_META, 2 * FFN_DIM), F32)),
        in_specs=[vmem] * 12 + [smem] + [vmem] * 14,
        out_specs=(vmem, vmem, vmem, vmem),
        scratch_shapes=[pltpu.VMEM((CONV_HALO + N_META, CONV_CH), F32)],
        compiler_params=pltpu.CompilerParams(vmem_limit_bytes=VMEM_LIMIT),
        name="meta_prologue",
    )(meta_tokens.astype(F32), gpre, wqT, row(bq), wk, bk, wvT, row(bv), wglu, bglu, wgate,
      bgate, sinks, wap, dww, dwb, lng, lnb, wcp, bcp, wout, gpost, gffn, wup,
      rc[:N_META], rs1[:N_META], rs2[:N_META])

    vmetaT = v_meta.T.astype(BF16)
    umeta_tail = u_meta[N_META - SUBLANES:, :]

    T = MIX_TILE
    bqT = jnp.broadcast_to(bq[:, None], (ATTN_WIDTH, T)).astype(F32)
    bvT = jnp.broadcast_to(bv[:, None], (KV_WIDTH, T)).astype(F32)
    tile_spec = pl.BlockSpec((1, T, D_MODEL), lambda b, t: (b, t, 0))
    rope_spec = pl.BlockSpec((T, LANES), lambda b, t: (t, 0))
    ropeT_spec = pl.BlockSpec((SUBLANES, T), lambda b, t: (0, t))
    mixer_in = [
        (x, tile_spec), (gpre, None), (wqT, None), (bqT, None), (wk, None), (bk, None),
        (wvT, None), (bvT, None), (wglu, None), (bglu, None), (wgate, None), (bgate, None),
        (sinks, smem), (wap, None), (dww8, None), (dwb, None), (lng, None), (lnb, None),
        (wcp, None), (bcp, None), (wout, None), (gpost, None),
        (rc[N_META:], rope_spec), (rs1[N_META:], rope_spec), (rs2[N_META:], rope_spec),
        (cosT[:, N_META:], ropeT_spec), (sinT[:, N_META:], ropeT_spec),
        (k_meta, None), (vmetaT, None), (glu_meta, None),
    ]
    h2 = pl.pallas_call(
        _mixer_kernel,
        out_shape=jax.ShapeDtypeStruct((bsz, seq, D_MODEL), F32),
        grid=(bsz, seq // T),
        in_specs=[spec if spec is not None else _const_spec(a.shape) for a, spec in mixer_in],
        out_specs=tile_spec,
        scratch_shapes=[
            pltpu.VMEM((4, BLOCK + T, LANES), BF16),
            pltpu.VMEM((KV_WIDTH, BLOCK + T), BF16),
            pltpu.VMEM((CONV_HALO + T, CONV_CH), F32),
            pltpu.VMEM((CONV_SHIFTS - 1, T + CONV_PAD, CONV_CH), F32),
            pltpu.VMEM((ATTN_WIDTH, T), BF16),
            pltpu.VMEM((T // BLOCK, ATTN_WIDTH, BLOCK), F32),
            pltpu.VMEM((T, D_MODEL), F32),
            pltpu.VMEM((T, CONV_CH), F32),
            pltpu.VMEM((2, 2 * BLOCK, 2 * BLOCK), F32),
        ],
        compiler_params=pltpu.CompilerParams(
            dimension_semantics=("arbitrary", "arbitrary"), vmem_limit_bytes=VMEM_LIMIT),
        name="mixer",
    )(*[a for a, _ in mixer_in])

    T2 = FFN_TILE
    tile2 = pl.BlockSpec((1, T2, D_MODEL), lambda b, t: (b, t, 0))
    ffn_in = [(h2, tile2), (gffn, None), (gffn_post, None), (wup, None), (dwp, None),
              (wdown, None), (umeta_tail, None)]
    out = pl.pallas_call(
        _ffn_kernel,
        out_shape=jax.ShapeDtypeStruct((bsz, seq, D_MODEL), F32),
        grid=(bsz, seq // T2),
        in_specs=[spec if spec is not None else _const_spec(a.shape) for a, spec in ffn_in],
        out_specs=tile2,
        scratch_shapes=[
            pltpu.VMEM((T2, D_MODEL), BF16),
            pltpu.VMEM((FFN_UP_AHEAD + 1, 2, SUBLANES + T2, FFN_CHUNK), F32),
            pltpu.VMEM((SUBLANES, 2 * FFN_DIM), F32),
            pltpu.VMEM((T2, FFN_DIM), BF16),
            pltpu.VMEM((T2, D_MODEL), F32),
        ],
        compiler_params=pltpu.CompilerParams(
            dimension_semantics=("arbitrary", "arbitrary"), vmem_limit_bytes=VMEM_LIMIT),
        name="conv_ffn",
    )(*[a for a, _ in ffn_in])
    return out.astype(x.dtype)
```
